```python
import math
import jax, jax.numpy as jnp
from jax import lax
import numpy as np

D_MODEL = 1024
BATCH = 8
SEQ = 4096
DEPTH = 4

N_A = DEPTH // 2
N_B = DEPTH - N_A
POOL_WINDOWS = (2, 4, 8, 16)
N_POOL_GROUPS = len(POOL_WINDOWS)
GROUP_CH = D_MODEL // N_POOL_GROUPS
HEAD_DIM = 64
N_Q_HEADS = D_MODEL // HEAD_DIM
N_KV_HEADS = 4
Q_PER_KV = N_Q_HEADS // N_KV_HEADS
WINDOW = 128
BLOCK = 128
ROPE_THETA = 10000.0
ATTN_SCALE = 1.0 / math.sqrt(HEAD_DIM)
NEG_INF = -1e30
D_FF = 2816
CONV_WIDTH = 3
RMS_EPS = 1e-6

kernel_name = "yoco_pool_swa_sink_hybrid"


def rms_norm(x, g):
    xf = x.astype(jnp.float32)
    y = xf * lax.rsqrt(jnp.mean(xf * xf, axis=-1, keepdims=True) + RMS_EPS)
    return (y * g.astype(jnp.float32)).astype(x.dtype)


def pool_mixer(h, w_pool, scale):
    B, S, D = h.shape
    hf = h.astype(jnp.float32)
    csum = jnp.concatenate([jnp.zeros((B, 1, D), jnp.float32), jnp.cumsum(hf, axis=1)], axis=1)
    t = jnp.arange(1, S + 1)
    diffs = []
    for gi, w in enumerate(POOL_WINDOWS):
        sl = slice(gi * GROUP_CH, (gi + 1) * GROUP_CH)
        lo = jnp.maximum(t - w, 0)
        cnt = jnp.minimum(t, w).astype(jnp.float32)
        mean = (csum[:, 1:, sl] - csum[:, lo, sl]) / cnt[None, :, None]
        diffs.append(mean - hf[..., sl])
    d = jnp.stack(diffs, axis=2).astype(h.dtype)
    y = jnp.einsum('bsgc,gcd->bsgd', d, w_pool).reshape(B, S, D)
    return y * scale


def conv_glu_ffn(h, w_in, conv_w, conv_b, w_out):
    S = h.shape[1]
    u = h @ w_in
    up = jnp.pad(u, ((0, 0), (CONV_WIDTH - 1, 0), (0, 0)))
    u = sum(conv_w[k] * up[:, k:k + S] for k in range(CONV_WIDTH)) + conv_b
    gate, val = jnp.split(u, 2, axis=-1)
    return (jax.nn.gelu(gate, approximate=True) * val) @ w_out


def rope(x, cos, sin):
    xf = x.astype(jnp.float32)
    x1, x2 = jnp.split(xf, 2, axis=-1)
    return jnp.concatenate([x1 * cos - x2 * sin, x2 * cos + x1 * sin], axis=-1).astype(x.dtype)


def rope_tables(positions):
    inv_freq = 1.0 / (ROPE_THETA ** (jnp.arange(0, HEAD_DIM, 2, dtype=jnp.float32) / HEAD_DIM))
    ang = positions.astype(jnp.float32)[..., None] * inv_freq
    return jnp.cos(ang)[:, :, None, :], jnp.sin(ang)[:, :, None, :]


def band_blocks(t):
    B, S = t.shape[:2]
    nb = S // BLOCK
    tb = t.reshape(B, nb, BLOCK, N_KV_HEADS, HEAD_DIM)
    prev = jnp.pad(tb, ((0, 0), (1, 0), (0, 0), (0, 0), (0, 0)))[:, :-1]
    return jnp.concatenate([prev, tb], axis=2).astype(jnp.float32)


def swa_sink_attention(q, kk, vv, sinks):
    B, S = q.shape[:2]
    nb = S // BLOCK
    qb = q.reshape(B, nb, BLOCK, N_KV_HEADS, Q_PER_KV, HEAD_DIM).astype(jnp.float32) * ATTN_SCALE
    s = jnp.einsum('bnqhgd,bnkhd->bnhgqk', qb, kk)
    qi = jnp.arange(BLOCK)[:, None]
    kj = jnp.arange(2 * BLOCK)[None, :]
    rel = BLOCK + qi - kj
    blk = jnp.arange(nb)[:, None, None]
    valid = (rel >= 0) & (rel < WINDOW) & (blk * BLOCK + kj - BLOCK >= 0)
    s = jnp.where(valid[None, :, None, None], s, NEG_INF)
    sink = sinks.astype(jnp.float32).reshape(N_KV_HEADS, Q_PER_KV)[None, None, :, :, None, None]
    m = jnp.maximum(jnp.max(s, axis=-1, keepdims=True), sink)
    p = jnp.exp(s - m)
    denom = jnp.sum(p, axis=-1) + jnp.exp(sink - m)[..., 0]
    o = jnp.einsum('bnhgqk,bnkhd->bnhgqd', p, vv) / denom[..., None]
    o = o.transpose(0, 1, 4, 2, 3, 5).reshape(B, S, N_Q_HEADS * HEAD_DIM)
    return o.astype(q.dtype)


def setup_inputs(seed: int = 0) -> dict:
    key = jax.random.key(seed)
    ks = jax.random.split(key, 20)
    f32 = jnp.float32
    D, F = D_MODEL, D_FF
    HQD, HKVD = N_Q_HEADS * HEAD_DIM, N_KV_HEADS * HEAD_DIM

    def gain(k, shape):
        return 1.0 + 0.05 * jax.random.normal(k, shape, f32)

    x = jax.random.normal(ks[0], (BATCH, SEQ, D), f32)
    positions = jnp.broadcast_to(jnp.arange(SEQ, dtype=jnp.int32)[None, :], (BATCH, SEQ))
    return {
        "x": x,
        "positions": positions,
        "mix_pre_g": gain(ks[1], (DEPTH, D)),
        "mix_post_g": gain(ks[2], (DEPTH, D)),
        "pool_w": jax.random.normal(ks[3], (N_A, N_POOL_GROUPS, GROUP_CH, GROUP_CH), f32) * GROUP_CH ** -0.5,
        "pool_scale": 1.0 + 0.1 * jax.random.normal(ks[4], (N_A, D), f32),
        "kv_norm_g": gain(ks[5], (D,)),
        "w_kv": jax.random.normal(ks[6], (D, 2 * HKVD), f32) * D ** -0.5,
        "w_q": jax.random.normal(ks[7], (N_B, D, HQD), f32) * D ** -0.5,
        "w_o": jax.random.normal(ks[8], (N_B, HQD, D), f32) * HQD ** -0.5,
        "sinks": jax.random.normal(ks[9], (N_B, N_Q_HEADS), f32),
        "ffn_pre_g": gain(ks[10], (DEPTH, D)),
        "ffn_post_g": gain(ks[11], (DEPTH, D)),
        "ffn_w_in": jax.random.normal(ks[12], (DEPTH, D, 2 * F), f32) * D ** -0.5,
        "ffn_conv_w": jax.random.normal(ks[13], (DEPTH, CONV_WIDTH, 2 * F), f32) * CONV_WIDTH ** -0.5,
        "ffn_conv_b": 0.01 * jax.random.normal(ks[14], (DEPTH, 2 * F), f32),
        "ffn_w_out": jax.random.normal(ks[15], (DEPTH, F, D), f32) * F ** -0.5,
    }


def reference(x, positions, mix_pre_g, mix_post_g, pool_w, pool_scale, kv_norm_g, w_kv,
              w_q, w_o, sinks, ffn_pre_g, ffn_post_g, ffn_w_in, ffn_conv_w, ffn_conv_b, ffn_w_out):
    B, S, D = x.shape
    cos, sin = rope_tables(positions)
    kk = vv = None
    for layer in range(DEPTH):
        h = rms_norm(x, mix_pre_g[layer])
        if layer < N_A:
            m = pool_mixer(h, pool_w[layer], pool_scale[layer])
        else:
            if layer == N_A:
                hkv = rms_norm(x, kv_norm_g)
                kv = (hkv @ w_kv).reshape(B, S, 2, N_KV_HEADS, HEAD_DIM)
                k_shared = rope(kv[:, :, 0], cos, sin)
                kk, vv = band_blocks(k_shared), band_blocks(kv[:, :, 1])
            j = layer - N_A
            q = rope((h @ w_q[j]).reshape(B, S, N_Q_HEADS, HEAD_DIM), cos, sin)
            m = swa_sink_attention(q, kk, vv, sinks[j]) @ w_o[j]
        x = x + rms_norm(m, mix_post_g[layer])
        f = conv_glu_ffn(rms_norm(x, ffn_pre_g[layer]), ffn_w_in[layer], ffn_conv_w[layer],
                         ffn_conv_b[layer], ffn_w_out[layer])
        x = x + rms_norm(f, ffn_post_g[layer])
    return x
```

```python
import functools
import math

import jax
import jax.numpy as jnp
from jax import lax
from jax.experimental import pallas as pl
from jax.experimental.pallas import tpu as pltpu

F32 = jnp.float32
BF16 = jnp.bfloat16

RMS_EPS = 1e-6
POOL_WINDOWS = (2, 4, 8, 16)
POOL_HALO = 16
HEAD_DIM = 64
HALF = HEAD_DIM // 2
N_Q_HEADS = 16
N_KV_HEADS = 4
Q_PER_KV = N_Q_HEADS // N_KV_HEADS
WINDOW = 128
BLOCK = 128
ROPE_THETA = 10000.0
ATTN_SCALE = 1.0 / math.sqrt(HEAD_DIM)
NEG_INF = -1e30
CONV_WIDTH = 3
CONV_HALO = 8
LANES = 128
FFN_CHUNK = 256
VMEM_LIMIT = 56 * 1024 * 1024

TILE_ROWS = 512

GELU_C = math.sqrt(2.0 / math.pi)


def _rms(x, g):
    ms = jnp.mean(x * x, axis=-1, keepdims=True)
    return x * lax.rsqrt(ms + RMS_EPS) * g


def _const_spec(shape):
    nd = len(shape)
    return pl.BlockSpec(shape, lambda b, s: (0,) * nd, pipeline_mode=pl.Buffered(1))


def _params():
    return pltpu.CompilerParams(dimension_semantics=("arbitrary", "arbitrary"),
                                vmem_limit_bytes=VMEM_LIMIT)


def _pool_kernel(x_ref, gpre_ref, gpost_ref, wp_ref, scale_ref, o_ref, hbuf, *, tm, gc):
    s = pl.program_id(1)

    @pl.when(s == 0)
    def _():
        hbuf[0:POOL_HALO, :] = jnp.zeros((POOL_HALO, hbuf.shape[1]), F32)

    x = x_ref[0]
    h = _rms(x, gpre_ref[...])
    hbuf[POOL_HALO:POOL_HALO + tm, :] = h
    t1 = lax.broadcasted_iota(jnp.int32, (tm, gc), 0) + (s * tm + 1)
    ys = []
    for gi, w in enumerate(POOL_WINDOWS):
        c0 = gi * gc
        acc = h[:, c0:c0 + gc]
        for k in range(1, w):
            acc = acc + hbuf[POOL_HALO - k:POOL_HALO - k + tm, c0:c0 + gc]
        cnt = jnp.minimum(t1, w).astype(F32)
        d = acc / cnt - h[:, c0:c0 + gc]
        ys.append(jnp.dot(d.astype(BF16), wp_ref[gi], preferred_element_type=F32))
    m = jnp.concatenate(ys, axis=-1) * scale_ref[...]
    o_ref[0] = x + _rms(m, gpost_ref[...])
    hbuf[0:POOL_HALO, :] = hbuf[tm:tm + POOL_HALO, :]


def _pool_layer(x, gpre, gpost, wp, scale, *, tm):
    B, S, D = x.shape
    gc = D // len(POOL_WINDOWS)
    xspec = pl.BlockSpec((1, tm, D), lambda b, s: (b, s, 0))
    return pl.pallas_call(
        functools.partial(_pool_kernel, tm=tm, gc=gc),
        grid=(B, S // tm),
        in_specs=[xspec, _const_spec((1, D)), _const_spec((1, D)),
                  _const_spec(wp.shape), _const_spec((1, D))],
        out_specs=xspec,
        out_shape=jax.ShapeDtypeStruct(x.shape, x.dtype),
        scratch_shapes=[pltpu.VMEM((tm + POOL_HALO, D), F32)],
        compiler_params=_params(),
        name="pool_layer",
    )(x, gpre, gpost, wp, scale)


def _ffn_kernel(x_ref, gpre_ref, gpost_ref, win_ref, cw_ref, cb_ref, wout_ref, o_ref,
                h_scr, act_scr, ubuf, halo, *, tm, fc, nc):
    s = pl.program_id(1)

    @pl.when(s == 0)
    def _():
        halo[...] = jnp.zeros(halo.shape, F32)

    x = x_ref[0]
    h_scr[...] = _rms(x, gpre_ref[...]).astype(BF16)
    for c in range(nc):
        lo, hi = c * 2 * fc, (c + 1) * 2 * fc
        u = jnp.dot(h_scr[...], win_ref[:, lo:hi], preferred_element_type=F32)
        ubuf[0:CONV_HALO, :] = halo[:, lo:hi]
        ubuf[CONV_HALO:CONV_HALO + tm, :] = u
        halo[:, lo:hi] = u[tm - CONV_HALO:tm, :]
        um1 = ubuf[CONV_HALO - 1:CONV_HALO - 1 + tm, :]
        um2 = ubuf[CONV_HALO - 2:CONV_HALO - 2 + tm, :]
        v = (cw_ref[0:1, lo:hi] * um2 + cw_ref[1:2, lo:hi] * um1 + cw_ref[2:3, lo:hi] * u
             + cb_ref[:, lo:hi])
        gate, val = v[:, :fc], v[:, fc:]
        act = 0.5 * gate * (1.0 + jnp.tanh(GELU_C * (gate + 0.044715 * (gate * gate * gate)))) * val
        act_scr[:, c * fc:(c + 1) * fc] = act.astype(BF16)
    f = jnp.dot(act_scr[...], wout_ref[...], preferred_element_type=F32)
    o_ref[0] = x + _rms(f, gpost_ref[...])


def _ffn_layer(x, gpre, gpost, win, cw, cb, wout, *, tm):
    B, S, D = x.shape
    F = wout.shape[0]
    fc = FFN_CHUNK
    nc = F // fc
    xspec = pl.BlockSpec((1, tm, D), lambda b, s: (b, s, 0))
    return pl.pallas_call(
        functools.partial(_ffn_kernel, tm=tm, fc=fc, nc=nc),
        grid=(B, S // tm),
        in_specs=[xspec, _const_spec((1, D)), _const_spec((1, D)), _const_spec(win.shape),
                  _const_spec(cw.shape), _const_spec(cb.shape), _const_spec(wout.shape)],
        out_specs=xspec,
        out_shape=jax.ShapeDtypeStruct(x.shape, x.dtype),
        scratch_shapes=[pltpu.VMEM((tm, D), BF16), pltpu.VMEM((tm, F), BF16),
                        pltpu.VMEM((tm + CONV_HALO, 2 * fc), F32),
                        pltpu.VMEM((CONV_HALO, 2 * F), F32)],
        compiler_params=_params(),
        name="ffn_layer",
    )(x, gpre, gpost, win, cw, cb, wout)


def _chunk_interleave(a, F, fc):
    lead = a.shape[:-1]
    a = a.reshape(*lead, 2, F // fc, fc)
    a = jnp.swapaxes(a, -3, -2)
    return a.reshape(*lead, 2 * F)


def _rope_kernel(pos_ref, invf_ref, sign_ref, cos_ref, sin_ref):
    ang = pos_ref[0].astype(F32) * invf_ref[...]
    cos_ref[0] = jnp.cos(ang)
    sin_ref[0] = jnp.sin(ang) * sign_ref[...]


def _rope_tables(positions, *, tm):
    B, S = positions.shape
    inv_freq = 1.0 / (ROPE_THETA ** (jnp.arange(0, HEAD_DIM, 2, dtype=F32) / HEAD_DIM))
    invf = jnp.tile(inv_freq, LANES // HALF)[None, :]
    sign = jnp.where(jnp.arange(LANES) < LANES // 2, -1.0, 1.0).astype(F32)[None, :]
    tspec = pl.BlockSpec((1, tm, LANES), lambda b, s: (b, s, 0))
    return pl.pallas_call(
        _rope_kernel,
        grid=(B, S // tm),
        in_specs=[pl.BlockSpec((1, tm, 1), lambda b, s: (b, s, 0)),
                  _const_spec((1, LANES)), _const_spec((1, LANES))],
        out_specs=[tspec, tspec],
        out_shape=[jax.ShapeDtypeStruct((B, S, LANES), F32)] * 2,
        compiler_params=_params(),
        name="rope_tables",
    )(positions[:, :, None], invf, sign)


def _rope(t, cos, sin):
    return t * cos + pltpu.roll(t, LANES // 2, 1) * sin


def _kv_kernel(x_ref, g_ref, cos_ref, sin_ref, wk_ref, wv_ref, k_ref, v_ref):
    h = _rms(x_ref[0], g_ref[...]).astype(BF16)
    kd = jnp.dot(h, wk_ref[...], preferred_element_type=F32)
    cos, sin = cos_ref[0], sin_ref[0]
    for j in range(N_KV_HEADS):
        k_ref[0, :, j * LANES:(j + 1) * LANES] = _rope(kd[:, j * LANES:(j + 1) * LANES], cos, sin).astype(BF16)
    v_ref[0] = jnp.dot(h, wv_ref[...], preferred_element_type=F32).astype(BF16)


def _kv_proj(x, g, cos, sin, wk, wv, *, tm):
    B, S, D = x.shape
    W = N_KV_HEADS * LANES
    tspec = pl.BlockSpec((1, tm, LANES), lambda b, s: (b, s, 0))
    ospec = pl.BlockSpec((1, tm, W), lambda b, s: (b, s, 0))
    return pl.pallas_call(
        _kv_kernel,
        grid=(B, S // tm),
        in_specs=[pl.BlockSpec((1, tm, D), lambda b, s: (b, s, 0)), _const_spec((1, D)), tspec, tspec,
                  _const_spec(wk.shape), _const_spec(wv.shape)],
        out_specs=[ospec, ospec],
        out_shape=[jax.ShapeDtypeStruct((B, S, W), BF16)] * 2,
        compiler_params=_params(),
        name="kv_proj",
    )(x, g, cos, sin, wk, wv)


def _attn_kernel(sinks_ref, x_ref, gpre_ref, gpost_ref, cos_ref, sin_ref, kc_ref, kp_ref, vc_ref, vp_ref,
                 wq_ref, wo_ref, o_ref, qst, kbuf, vbuf, o_scr, *, tm):
    s_idx = pl.program_id(1)
    x = x_ref[0]
    h = _rms(x, gpre_ref[...]).astype(BF16)
    q = jnp.dot(h, wq_ref[...], preferred_element_type=F32)
    cos, sin = cos_ref[0], sin_ref[0]
    lane = lax.broadcasted_iota(jnp.int32, (1, LANES), 1)
    first_of_pair = (lane % HEAD_DIM) < HALF
    for p in range(N_Q_HEADS // 2):
        qr = _rope(q[:, p * LANES:(p + 1) * LANES], cos, sin) * ATTN_SCALE
        qst[2 * p] = jnp.where(first_of_pair, qr, 0.0).astype(BF16)
        qst[2 * p + 1] = jnp.where(first_of_pair, 0.0, qr).astype(BF16)

    kbuf[0:BLOCK, :] = kp_ref[0]
    kbuf[BLOCK:BLOCK + tm, :] = kc_ref[0]
    vbuf[0:BLOCK, :] = vp_ref[0]
    vbuf[BLOCK:BLOCK + tm, :] = vc_ref[0]

    qi = lax.broadcasted_iota(jnp.int32, (BLOCK, 2 * BLOCK), 0)
    kj = lax.broadcasted_iota(jnp.int32, (BLOCK, 2 * BLOCK), 1)
    rel = BLOCK + qi - kj
    band = (rel >= 0) & (rel < WINDOW)
    out_lo = lax.broadcasted_iota(jnp.int32, (BLOCK, LANES), 1) < HEAD_DIM

    def block_body(jb, carry):
        r0 = pl.multiple_of(jb * BLOCK, BLOCK)
        seq_start = (s_idx * tm + r0 == 0).astype(jnp.int32)
        valid = band & (kj >= BLOCK * seq_start)
        for j in range(N_KV_HEADS):
            qblk = jnp.concatenate([qst[Q_PER_KV * j + i, pl.ds(r0, BLOCK), :] for i in range(Q_PER_KV)], axis=0)
            kblk = kbuf[pl.ds(r0, 2 * BLOCK), j * LANES:(j + 1) * LANES]
            sc = lax.dot_general(qblk, kblk, (((1,), (1,)), ((), ())), preferred_element_type=F32)
            ps, inv = [], []
            for i in range(Q_PER_KV):
                sink = sinks_ref[Q_PER_KV * j + i]
                sh = jnp.where(valid, sc[i * BLOCK:(i + 1) * BLOCK], NEG_INF)
                m = jnp.maximum(jnp.max(sh, axis=-1, keepdims=True), sink)
                pe = jnp.exp(sh - m)
                den = jnp.sum(pe, axis=-1, keepdims=True) + jnp.exp(sink - m)
                ps.append(pe.astype(BF16))
                inv.append(1.0 / den)
            vblk = vbuf[pl.ds(r0, 2 * BLOCK), j * LANES:(j + 1) * LANES]
            ov = jnp.dot(jnp.concatenate(ps, axis=0), vblk, preferred_element_type=F32)
            for pp in range(Q_PER_KV // 2):
                a = ov[(2 * pp) * BLOCK:(2 * pp + 1) * BLOCK] * inv[2 * pp]
                b = ov[(2 * pp + 1) * BLOCK:(2 * pp + 2) * BLOCK] * inv[2 * pp + 1]
                col = (Q_PER_KV // 2 * j + pp) * LANES
                o_scr[pl.ds(r0, BLOCK), col:col + LANES] = jnp.where(out_lo, a, b).astype(BF16)
        return carry

    lax.fori_loop(0, tm // BLOCK, block_body, 0)
    mo = jnp.dot(o_scr[...], wo_ref[...], preferred_element_type=F32)
    o_ref[0] = x + _rms(mo, gpost_ref[...])


def _attn_layer(x, gpre, gpost, cos, sin, k, v, wq, wo, sinks, *, tm):
    B, S, D = x.shape
    W = N_KV_HEADS * LANES
    nbt = tm // BLOCK
    xspec = pl.BlockSpec((1, tm, D), lambda b, s: (b, s, 0))
    tspec = pl.BlockSpec((1, tm, LANES), lambda b, s: (b, s, 0))
    cur = pl.BlockSpec((1, tm, W), lambda b, s: (b, s, 0))
    prev = pl.BlockSpec((1, BLOCK, W), lambda b, s: (b, jnp.maximum(s * nbt - 1, 0), 0))
    return pl.pallas_call(
        functools.partial(_attn_kernel, tm=tm),
        grid=(B, S // tm),
        in_specs=[pl.BlockSpec(memory_space=pltpu.SMEM), xspec, _const_spec((1, D)), _const_spec((1, D)),
                  tspec, tspec, cur, prev, cur, prev, _const_spec(wq.shape), _const_spec(wo.shape)],
        out_specs=xspec,
        out_shape=jax.ShapeDtypeStruct(x.shape, x.dtype),
        scratch_shapes=[pltpu.VMEM((N_Q_HEADS, tm, LANES), BF16),
                        pltpu.VMEM((tm + BLOCK, W), BF16), pltpu.VMEM((tm + BLOCK, W), BF16),
                        pltpu.VMEM((tm, D), BF16)],
        compiler_params=_params(),
        name="attn_layer",
    )(sinks, x, gpre, gpost, cos, sin, k, k, v, v, wq, wo)


def _pair_layout_q(wq):
    D = wq.shape[0]
    w = wq.reshape(D, N_Q_HEADS // 2, 2, 2, HALF)
    w = jnp.swapaxes(w, 2, 3)
    return w.reshape(D, N_Q_HEADS * HEAD_DIM)


def _dup_layout_k(wk):
    D = wk.shape[0]
    w = wk.reshape(D, N_KV_HEADS, 2, 1, HALF)
    w = jnp.broadcast_to(w, (D, N_KV_HEADS, 2, 2, HALF))
    return w.reshape(D, N_KV_HEADS * LANES)


def _dup_layout_v(wv):
    D = wv.shape[0]
    w = wv.reshape(D, N_KV_HEADS, 1, HEAD_DIM)
    w = jnp.broadcast_to(w, (D, N_KV_HEADS, 2, HEAD_DIM))
    return w.reshape(D, N_KV_HEADS * LANES)


def kernel(x, positions, mix_pre_g, mix_post_g, pool_w, pool_scale, kv_norm_g, w_kv, w_q, w_o, sinks,
           ffn_pre_g, ffn_post_g, ffn_w_in, ffn_conv_w, ffn_conv_b, ffn_w_out):
    B, S, D = x.shape
    tm = min(TILE_ROWS, S)
    depth = mix_pre_g.shape[0]
    n_a = pool_w.shape[0]
    F = ffn_w_out.shape[1]
    hkvd = N_KV_HEADS * HEAD_DIM
    assert S % tm == 0 and tm % BLOCK == 0 and F % FFN_CHUNK == 0

    cos = sin = k = v = None
    for layer in range(depth):
        gpre, gpost = mix_pre_g[layer][None, :], mix_post_g[layer][None, :]
        if layer < n_a:
            x = _pool_layer(x, gpre, gpost, pool_w[layer].astype(BF16), pool_scale[layer][None, :], tm=tm)
        else:
            if layer == n_a:
                cos, sin = _rope_tables(positions, tm=tm)
                k, v = _kv_proj(x, kv_norm_g[None, :], cos, sin,
                                _dup_layout_k(w_kv[:, :hkvd]).astype(BF16),
                                _dup_layout_v(w_kv[:, hkvd:]).astype(BF16), tm=tm)
            j = layer - n_a
            x = _attn_layer(x, gpre, gpost, cos, sin, k, v, _pair_layout_q(w_q[j]).astype(BF16),
                            w_o[j].astype(BF16), sinks[j], tm=tm)
        x = _ffn_layer(x, ffn_pre_g[layer][None, :], ffn_post_g[layer][None, :],
                       _chunk_interleave(ffn_w_in[layer], F, FFN_CHUNK).astype(BF16),
                       _chunk_interleave(ffn_conv_w[layer], F, FFN_CHUNK),
                       _chunk_interleave(ffn_conv_b[layer], F, FFN_CHUNK)[None, :],
                       ffn_w_out[layer].astype(BF16), tm=tm)
    return x
```

```python
import functools
import math

import jax
import jax.numpy as jnp
from jax import lax
from jax.experimental import pallas as pl
from jax.experimental.pallas import tpu as pltpu

F32 = jnp.float32
BF16 = jnp.bfloat16

RMS_EPS = 1e-6
POOL_WINDOWS = (2, 4, 8, 16)
POOL_HALO = 16
HEAD_DIM = 64
HALF = HEAD_DIM // 2
N_Q_HEADS = 16
N_KV_HEADS = 4
Q_PER_KV = N_Q_HEADS // N_KV_HEADS
WINDOW = 128
BLOCK = 128
ROPE_THETA = 10000.0
ATTN_SCALE = 1.0 / math.sqrt(HEAD_DIM)
NEG_INF = -1e30
CONV_WIDTH = 3
LANES = 128
SUBLANES = 8
STAGE_PAD = 8
FFN_CHUNK = 256
VMEM_LIMIT = 56 * 1024 * 1024

TILE_ROWS = 512

GELU_C = math.sqrt(2.0 / math.pi)
LOG2E = math.log2(math.e)


def _rms(x, g):
    ms = jnp.mean(x * x, axis=-1, keepdims=True)
    return x * lax.rsqrt(ms + RMS_EPS) * g


def _const_spec(shape):
    nd = len(shape)
    return pl.BlockSpec(shape, lambda b, s: (0,) * nd, pipeline_mode=pl.Buffered(1))


def _params():
    return pltpu.CompilerParams(dimension_semantics=("arbitrary", "arbitrary"),
                                vmem_limit_bytes=VMEM_LIMIT)


def _pool_kernel(x_ref, gpre_ref, gpost_ref, wp_ref, scale_ref, o_ref, hbuf, *, tm, gc):
    s = pl.program_id(1)

    @pl.when(s == 0)
    def _():
        hbuf[0:POOL_HALO, :] = jnp.zeros((POOL_HALO, hbuf.shape[1]), F32)

    x = x_ref[0]
    h = _rms(x, gpre_ref[...])
    hbuf[POOL_HALO:POOL_HALO + tm, :] = h
    t1 = lax.broadcasted_iota(jnp.int32, (tm, gc), 0) + (s * tm + 1)
    ys = []
    for gi, w in enumerate(POOL_WINDOWS):
        c0 = gi * gc
        acc = h[:, c0:c0 + gc]
        for k in range(1, w):
            acc = acc + hbuf[POOL_HALO - k:POOL_HALO - k + tm, c0:c0 + gc]
        cnt = jnp.minimum(t1, w).astype(F32)
        d = acc / cnt - h[:, c0:c0 + gc]
        ys.append(jnp.dot(d.astype(BF16), wp_ref[gi], preferred_element_type=F32))
    m = jnp.concatenate(ys, axis=-1) * scale_ref[...]
    o_ref[0] = x + _rms(m, gpost_ref[...])
    hbuf[0:POOL_HALO, :] = hbuf[tm:tm + POOL_HALO, :]


def _pool_layer(x, gpre, gpost, wp, scale, *, tm):
    B, S, D = x.shape
    gc = D // len(POOL_WINDOWS)
    xspec = pl.BlockSpec((1, tm, D), lambda b, s: (b, s, 0))
    return pl.pallas_call(
        functools.partial(_pool_kernel, tm=tm, gc=gc),
        grid=(B, S // tm),
        in_specs=[xspec, _const_spec((1, D)), _const_spec((1, D)),
                  _const_spec(wp.shape), _const_spec((1, D))],
        out_specs=xspec,
        out_shape=jax.ShapeDtypeStruct(x.shape, x.dtype),
        scratch_shapes=[pltpu.VMEM((tm + POOL_HALO, D), F32)],
        compiler_params=_params(),
        name="pool_layer",
    )(x, gpre, gpost, wp, scale)


def _gelu_gate(gate, val):
    inner = gate * (GELU_C + (GELU_C * 0.044715) * (gate * gate))
    return (gate * val) * (0.5 * jnp.tanh(inner) + 0.5)


def _ffn_kernel(x_ref, gpre_ref, gpost_ref, win_ref, cw_ref, cb_ref, wout_ref, o_ref,
                stage, h_scr, act_scr, halo, *, tm, fc, nc):
    F = nc * fc
    D = wout_ref.shape[1]
    Q = tm // SUBLANES
    nslab = D // LANES

    @pl.when(pl.program_id(1) == 0)
    def _():
        halo[...] = jnp.zeros(halo.shape, F32)

    x = x_ref[0]
    h = _rms(x, gpre_ref[...])
    P = Q + STAGE_PAD
    for j in range(nslab):
        for i in range(SUBLANES):
            stage[j, i * P:i * P + Q, :] = h[i * Q:(i + 1) * Q, j * LANES:(j + 1) * LANES]
    for k in range(0, Q, 2):
        blk = [jnp.concatenate([stage[j, pl.ds(k + d, SUBLANES, stride=P), :] for j in range(nslab)], axis=1)
               for d in range(2)]
        h_scr[k * SUBLANES:(k + 2) * SUBLANES, :] = jnp.concatenate(blk, axis=0).astype(BF16)

    sub = lax.broadcasted_iota(jnp.int32, (SUBLANES, fc), 0)

    def conv(col):
        u = jnp.dot(h_scr[...], win_ref[:, col:col + fc], preferred_element_type=F32)
        last1, last2 = u[tm - SUBLANES:tm, :], u[tm - 2 * SUBLANES:tm - SUBLANES, :]
        b1 = pltpu.roll(jnp.where(sub == SUBLANES - 1, halo[0:SUBLANES, col:col + fc], last1), 1, 0)
        b2 = pltpu.roll(jnp.where(sub == SUBLANES - 1, halo[SUBLANES:2 * SUBLANES, col:col + fc], last2), 1, 0)
        halo[0:SUBLANES, col:col + fc] = last1
        halo[SUBLANES:2 * SUBLANES, col:col + fc] = last2
        um1 = jnp.concatenate([b1, u[:tm - SUBLANES, :]], axis=0)
        um2 = jnp.concatenate([b2, b1, u[:tm - 2 * SUBLANES, :]], axis=0)
        return (cw_ref[0:1, col:col + fc] * um2 + cw_ref[1:2, col:col + fc] * um1
                + cw_ref[2:3, col:col + fc] * u + cb_ref[:, col:col + fc])

    for c in range(nc):
        gate = conv(c * fc)
        val = conv(F + c * fc)
        act_scr[:, c * fc:(c + 1) * fc] = _gelu_gate(gate, val).astype(BF16)

    f = jnp.dot(act_scr[...], wout_ref[...], preferred_element_type=F32)
    for k in range(Q):
        for j in range(nslab):
            stage[j, pl.ds(k, SUBLANES, stride=P), :] = f[k * SUBLANES:(k + 1) * SUBLANES, j * LANES:(j + 1) * LANES]
    fn = jnp.concatenate([jnp.concatenate([stage[j, i * P:i * P + Q, :] for i in range(SUBLANES)], axis=0)
                          for j in range(nslab)], axis=1)
    o_ref[0] = x + _rms(fn, gpost_ref[...])


def _ffn_layer(x, gpre, gpost, win, cw, cb, wout, *, tm):
    B, S, D = x.shape
    F = wout.shape[0]
    fc = FFN_CHUNK
    nc = F // fc
    assert cw.shape == (CONV_WIDTH, 2 * F) and tm % (2 * SUBLANES) == 0
    xspec = pl.BlockSpec((1, tm, D), lambda b, s: (b, s, 0))
    return pl.pallas_call(
        functools.partial(_ffn_kernel, tm=tm, fc=fc, nc=nc),
        grid=(B, S // tm),
        in_specs=[xspec, _const_spec((1, D)), _const_spec((1, D)), _const_spec(win.shape),
                  _const_spec(cw.shape), _const_spec(cb.shape), _const_spec(wout.shape)],
        out_specs=xspec,
        out_shape=jax.ShapeDtypeStruct(x.shape, x.dtype),
        scratch_shapes=[pltpu.VMEM((D // LANES, tm + SUBLANES * STAGE_PAD, LANES), F32),
                        pltpu.VMEM((tm, D), BF16), pltpu.VMEM((tm, F), BF16),
                        pltpu.VMEM(((CONV_WIDTH - 1) * SUBLANES, 2 * F), F32)],
        compiler_params=_params(),
        name="ffn_layer",
    )(x, gpre, gpost, win, cw, cb, wout)


def _rope_kernel(pos_ref, invf_ref, sign_ref, cos_ref, sin_ref):
    ang = pos_ref[0].astype(F32) * invf_ref[...]
    cos_ref[0] = jnp.cos(ang)
    sin_ref[0] = jnp.sin(ang) * sign_ref[...]


def _rope_tables(positions, *, tm):
    B, S = positions.shape
    inv_freq = 1.0 / (ROPE_THETA ** (jnp.arange(0, HEAD_DIM, 2, dtype=F32) / HEAD_DIM))
    invf = jnp.tile(inv_freq, LANES // HALF)[None, :]
    sign = jnp.where(jnp.arange(LANES) < LANES // 2, -1.0, 1.0).astype(F32)[None, :]
    tspec = pl.BlockSpec((1, tm, LANES), lambda b, s: (b, s, 0))
    return pl.pallas_call(
        _rope_kernel,
        grid=(B, S // tm),
        in_specs=[pl.BlockSpec((1, tm, 1), lambda b, s: (b, s, 0)),
                  _const_spec((1, LANES)), _const_spec((1, LANES))],
        out_specs=[tspec, tspec],
        out_shape=[jax.ShapeDtypeStruct((B, S, LANES), F32)] * 2,
        compiler_params=_params(),
        name="rope_tables",
    )(positions[:, :, None], invf, sign)


def _rope(t, cos, sin):
    return t * cos + pltpu.roll(t, LANES // 2, 1) * sin


def _kv_kernel(x_ref, g_ref, cos_ref, sin_ref, wk_ref, wv_ref, k_ref, v_ref):
    h = _rms(x_ref[0], g_ref[...]).astype(BF16)
    kd = jnp.dot(h, wk_ref[...], preferred_element_type=F32)
    cos, sin = cos_ref[0], sin_ref[0]
    for j in range(N_KV_HEADS):
        k_ref[0, :, j * LANES:(j + 1) * LANES] = _rope(kd[:, j * LANES:(j + 1) * LANES], cos, sin).astype(BF16)
    v_ref[0] = jnp.dot(h, wv_ref[...], preferred_element_type=F32).astype(BF16)


def _kv_proj(x, g, cos, sin, wk, wv, *, tm):
    B, S, D = x.shape
    W = N_KV_HEADS * LANES
    tspec = pl.BlockSpec((1, tm, LANES), lambda b, s: (b, s, 0))
    ospec = pl.BlockSpec((1, tm, W), lambda b, s: (b, s, 0))
    return pl.pallas_call(
        _kv_kernel,
        grid=(B, S // tm),
        in_specs=[pl.BlockSpec((1, tm, D), lambda b, s: (b, s, 0)), _const_spec((1, D)), tspec, tspec,
                  _const_spec(wk.shape), _const_spec(wv.shape)],
        out_specs=[ospec, ospec],
        out_shape=[jax.ShapeDtypeStruct((B, S, W), BF16)] * 2,
        compiler_params=_params(),
        name="kv_proj",
    )(x, g, cos, sin, wk, wv)


def _attn_kernel(sinks_ref, x_ref, gpre_ref, gpost_ref, cos_ref, sin_ref, kc_ref, kp_ref, vc_ref, vp_ref,
                 wq_ref, wo_ref, o_ref, qst, kbuf, vbuf, o_scr, *, tm):
    s_idx = pl.program_id(1)
    x = x_ref[0]
    h = _rms(x, gpre_ref[...]).astype(BF16)
    q = jnp.dot(h, wq_ref[...], preferred_element_type=F32)
    cos, sin = cos_ref[0], sin_ref[0]
    lane = lax.broadcasted_iota(jnp.int32, (1, LANES), 1)
    first_of_pair = (lane % HEAD_DIM) < HALF
    for p in range(N_Q_HEADS // 2):
        qr = _rope(q[:, p * LANES:(p + 1) * LANES], cos, sin) * (ATTN_SCALE * LOG2E)
        qst[2 * p] = jnp.where(first_of_pair, qr, 0.0).astype(BF16)
        qst[2 * p + 1] = jnp.where(first_of_pair, 0.0, qr).astype(BF16)

    kbuf[0:BLOCK, :] = kp_ref[0]
    kbuf[BLOCK:BLOCK + tm, :] = kc_ref[0]
    for j in range(N_KV_HEADS):
        vbuf[0:BLOCK, 2 * j * LANES:(2 * j + 1) * LANES] = vp_ref[0, :, j * LANES:(j + 1) * LANES]
        vbuf[BLOCK:BLOCK + tm, 2 * j * LANES:(2 * j + 1) * LANES] = vc_ref[0, :, j * LANES:(j + 1) * LANES]
        vbuf[:, (2 * j + 1) * LANES:(2 * j + 2) * LANES] = jnp.ones((tm + BLOCK, LANES), BF16)

    qi = lax.broadcasted_iota(jnp.int32, (BLOCK, 2 * BLOCK), 0)
    kj = lax.broadcasted_iota(jnp.int32, (BLOCK, 2 * BLOCK), 1)
    rel = BLOCK + qi - kj
    band = (rel >= 0) & (rel < WINDOW)
    out_lo = lax.broadcasted_iota(jnp.int32, (BLOCK, LANES), 1) < HEAD_DIM

    def block_body(jb, carry):
        r0 = pl.multiple_of(jb * BLOCK, BLOCK)
        seq_start = (s_idx * tm + r0 == 0).astype(jnp.int32)
        valid = band & (kj >= BLOCK * seq_start)
        for j in range(N_KV_HEADS):
            qblk = jnp.concatenate([qst[Q_PER_KV * j + i, pl.ds(r0, BLOCK), :] for i in range(Q_PER_KV)], axis=0)
            kblk = kbuf[pl.ds(r0, 2 * BLOCK), j * LANES:(j + 1) * LANES]
            sc = lax.dot_general(qblk, kblk, (((1,), (1,)), ((), ())), preferred_element_type=F32)
            ps, sink_p = [], []
            for i in range(Q_PER_KV):
                sink = sinks_ref[Q_PER_KV * j + i] * LOG2E
                sh = jnp.where(valid, sc[i * BLOCK:(i + 1) * BLOCK], NEG_INF)
                m = jnp.maximum(jnp.max(sh, axis=-1, keepdims=True), sink)
                ps.append(jnp.exp2(sh - m).astype(BF16))
                sink_p.append(jnp.exp2(sink - m))
            vblk = vbuf[pl.ds(r0, 2 * BLOCK), 2 * j * LANES:(2 * j + 2) * LANES]
            ov = jnp.dot(jnp.concatenate(ps, axis=0), vblk, preferred_element_type=F32)
            heads = []
            for i in range(Q_PER_KV):
                num = ov[i * BLOCK:(i + 1) * BLOCK, :LANES]
                den = ov[i * BLOCK:(i + 1) * BLOCK, LANES:] + sink_p[i]
                heads.append(num / den)
            for pp in range(Q_PER_KV // 2):
                col = (Q_PER_KV // 2 * j + pp) * LANES
                o_scr[pl.ds(r0, BLOCK), col:col + LANES] = jnp.where(out_lo, heads[2 * pp], heads[2 * pp + 1]).astype(BF16)
        return carry

    lax.fori_loop(0, tm // BLOCK, block_body, 0)
    mo = jnp.dot(o_scr[...], wo_ref[...], preferred_element_type=F32)
    o_ref[0] = x + _rms(mo, gpost_ref[...])


def _attn_layer(x, gpre, gpost, cos, sin, k, v, wq, wo, sinks, *, tm):
    B, S, D = x.shape
    W = N_KV_HEADS * LANES
    nbt = tm // BLOCK
    xspec = pl.BlockSpec((1, tm, D), lambda b, s: (b, s, 0))
    tspec = pl.BlockSpec((1, tm, LANES), lambda b, s: (b, s, 0))
    cur = pl.BlockSpec((1, tm, W), lambda b, s: (b, s, 0))
    prev = pl.BlockSpec((1, BLOCK, W), lambda b, s: (b, jnp.maximum(s * nbt - 1, 0), 0))
    return pl.pallas_call(
        functools.partial(_attn_kernel, tm=tm),
        grid=(B, S // tm),
        in_specs=[pl.BlockSpec(memory_space=pltpu.SMEM), xspec, _const_spec((1, D)), _const_spec((1, D)),
                  tspec, tspec, cur, prev, cur, prev, _const_spec(wq.shape), _const_spec(wo.shape)],
        out_specs=xspec,
        out_shape=jax.ShapeDtypeStruct(x.shape, x.dtype),
        scratch_shapes=[pltpu.VMEM((N_Q_HEADS, tm, LANES), BF16),
                        pltpu.VMEM((tm + BLOCK, W), BF16), pltpu.VMEM((tm + BLOCK, 2 * W), BF16),
                        pltpu.VMEM((tm, D), BF16)],
        compiler_params=_params(),
        name="attn_layer",
    )(sinks, x, gpre, gpost, cos, sin, k, k, v, v, wq, wo)


def _pair_layout_q(wq):
    D = wq.shape[0]
    w = wq.reshape(D, N_Q_HEADS // 2, 2, 2, HALF)
    w = jnp.swapaxes(w, 2, 3)
    return w.reshape(D, N_Q_HEADS * HEAD_DIM)


def _dup_layout_k(wk):
    D = wk.shape[0]
    w = wk.reshape(D, N_KV_HEADS, 2, 1, HALF)
    w = jnp.broadcast_to(w, (D, N_KV_HEADS, 2, 2, HALF))
    return w.reshape(D, N_KV_HEADS * LANES)


def _dup_layout_v(wv):
    D = wv.shape[0]
    w = wv.reshape(D, N_KV_HEADS, 1, HEAD_DIM)
    w = jnp.broadcast_to(w, (D, N_KV_HEADS, 2, HEAD_DIM))
    return w.reshape(D, N_KV_HEADS * LANES)


def kernel(x, positions, mix_pre_g, mix_post_g, pool_w, pool_scale, kv_norm_g, w_kv, w_q, w_o, sinks,
           ffn_pre_g, ffn_post_g, ffn_w_in, ffn_conv_w, ffn_conv_b, ffn_w_out):
    B, S, D = x.shape
    tm = min(TILE_ROWS, S)
    depth = mix_pre_g.shape[0]
    n_a = pool_w.shape[0]
    F = ffn_w_out.shape[1]
    hkvd = N_KV_HEADS * HEAD_DIM
    assert S % tm == 0 and tm % BLOCK == 0 and F % FFN_CHUNK == 0

    cos = sin = k = v = None
    for layer in range(depth):
        gpre, gpost = mix_pre_g[layer][None, :], mix_post_g[layer][None, :]
        if layer < n_a:
            x = _pool_layer(x, gpre, gpost, pool_w[layer].astype(BF16), pool_scale[layer][None, :], tm=tm)
        else:
            if layer == n_a:
                cos, sin = _rope_tables(positions, tm=tm)
                k, v = _kv_proj(x, kv_norm_g[None, :], cos, sin,
                                _dup_layout_k(w_kv[:, :hkvd]).astype(BF16),
                                _dup_layout_v(w_kv[:, hkvd:]).astype(BF16), tm=tm)
            j = layer - n_a
            x = _attn_layer(x, gpre, gpost, cos, sin, k, v, _pair_layout_q(w_q[j]).astype(BF16),
                            w_o[j].astype(BF16), sinks[j], tm=tm)
        x = _ffn_layer(x, ffn_pre_g[layer][None, :], ffn_post_g[layer][None, :],
                       ffn_w_in[layer].astype(BF16), ffn_conv_w[layer], ffn_conv_b[layer][None, :],
                       ffn_w_out[layer].astype(BF16), tm=tm)
    return x
```

```python
import functools
import math

import jax
import jax.numpy as jnp
from jax import lax
from jax.experimental import pallas as pl
from jax.experimental.pallas import tpu as pltpu

F32 = jnp.float32
BF16 = jnp.bfloat16

RMS_EPS = 1e-6
POOL_WINDOWS = (2, 4, 8, 16)
HEAD_DIM = 64
HALF = HEAD_DIM // 2
N_Q_HEADS = 16
N_KV_HEADS = 4
Q_PER_KV = N_Q_HEADS // N_KV_HEADS
WINDOW = 128
BLOCK = 128
ROPE_THETA = 10000.0
ATTN_SCALE = 1.0 / math.sqrt(HEAD_DIM)
NEG_INF = -1e30
CONV_WIDTH = 3
LANES = 128
SUBLANES = 8
STAGE_PAD = 8
FFN_CHUNK = 256
FFN_SUBTILES = 2
VMEM_LIMIT = 56 * 1024 * 1024

TILE_ROWS = 512
FFN_TILE_ROWS = 1024

GELU_C = math.sqrt(2.0 / math.pi)
LOG2E = math.log2(math.e)


def _rms(x, g):
    ms = jnp.mean(x * x, axis=-1, keepdims=True)
    return x * lax.rsqrt(ms + RMS_EPS) * g


def _const_spec(shape):
    nd = len(shape)
    return pl.BlockSpec(shape, lambda b, s: (0,) * nd, pipeline_mode=pl.Buffered(1))


def _params():
    return pltpu.CompilerParams(dimension_semantics=("arbitrary", "arbitrary"),
                                vmem_limit_bytes=VMEM_LIMIT)


def _pool_kernel(x_ref, gpre_ref, gpost_ref, wp_ref, scale_ref, o_ref, stage, hp, hprev, *, tm, gc):
    s_idx = pl.program_id(1)
    D = hp.shape[1]
    Q = tm // SUBLANES
    P = Q + STAGE_PAD
    nslab = D // LANES
    wmax = max(POOL_WINDOWS)
    tail = slice(tm - wmax * SUBLANES, tm)

    @pl.when(s_idx == 0)
    def _():
        hprev[...] = jnp.zeros(hprev.shape, F32)

    x = x_ref[0]
    h = _rms(x, gpre_ref[...])
    for j in range(nslab):
        for i in range(SUBLANES):
            stage[j, i * P:i * P + Q, :] = h[i * Q:(i + 1) * Q, j * LANES:(j + 1) * LANES]
    for k in range(Q):
        hp[k * SUBLANES:(k + 1) * SUBLANES, :] = jnp.concatenate(
            [stage[j, pl.ds(k, SUBLANES, stride=P), :] for j in range(nslab)], axis=1)

    sub = lax.broadcasted_iota(jnp.int32, (wmax * SUBLANES, D), 0) % SUBLANES
    wrap = jnp.where(sub == SUBLANES - 1, hprev[...], hp[tail, :])
    wrap = jnp.concatenate([pltpu.roll(wrap[m * SUBLANES:(m + 1) * SUBLANES, :], 1, 0) for m in range(wmax)], axis=0)
    hprev[...] = hp[tail, :]

    first = (s_idx == 0).astype(F32)
    sub8 = lax.broadcasted_iota(jnp.int32, (SUBLANES, gc), 0)
    ys = []
    for gi, w in enumerate(POOL_WINDOWS):
        cols = slice(gi * gc, (gi + 1) * gc)
        own = hp[:, cols]
        acc = own
        for sft in range(1, w):
            acc = acc + jnp.concatenate([wrap[(wmax - sft) * SUBLANES:, cols], hp[0:tm - sft * SUBLANES, cols]], axis=0)
        mean = acc * (1.0 / w)
        fix = [jnp.where(sub8 == 0, 1.0 + first * (w / (k + 1.0) - 1.0), 1.0) for k in range(w - 1)]
        head = jnp.concatenate(fix, axis=0) * mean[0:(w - 1) * SUBLANES, :]
        mean = jnp.concatenate([head, mean[(w - 1) * SUBLANES:, :]], axis=0)
        d = mean - own
        ys.append(jnp.dot(d.astype(BF16), wp_ref[gi], preferred_element_type=F32))
    m = jnp.concatenate(ys, axis=-1) * scale_ref[...]
    r = _rms(m, gpost_ref[...])
    for k in range(Q):
        for j in range(nslab):
            stage[j, pl.ds(k, SUBLANES, stride=P), :] = r[k * SUBLANES:(k + 1) * SUBLANES, j * LANES:(j + 1) * LANES]
    rn = jnp.concatenate([jnp.concatenate([stage[j, i * P:i * P + Q, :] for i in range(SUBLANES)], axis=0)
                          for j in range(nslab)], axis=1)
    o_ref[0] = x + rn


def _pool_layer(x, gpre, gpost, wp, scale, *, tm):
    B, S, D = x.shape
    gc = D // len(POOL_WINDOWS)
    wmax = max(POOL_WINDOWS)
    assert tm // SUBLANES >= wmax
    xspec = pl.BlockSpec((1, tm, D), lambda b, s: (b, s, 0))
    return pl.pallas_call(
        functools.partial(_pool_kernel, tm=tm, gc=gc),
        grid=(B, S // tm),
        in_specs=[xspec, _const_spec((1, D)), _const_spec((1, D)),
                  _const_spec(wp.shape), _const_spec((1, D))],
        out_specs=xspec,
        out_shape=jax.ShapeDtypeStruct(x.shape, x.dtype),
        scratch_shapes=[pltpu.VMEM((D // LANES, tm + SUBLANES * STAGE_PAD, LANES), F32),
                        pltpu.VMEM((tm, D), F32), pltpu.VMEM((wmax * SUBLANES, D), F32)],
        compiler_params=_params(),
        name="pool_layer",
    )(x, gpre, gpost, wp, scale)


def _gelu_gate(gate, val):
    inner = gate * (GELU_C + (GELU_C * 0.044715) * (gate * gate))
    return (gate * val) * (0.5 * jnp.tanh(inner) + 0.5)


def _ffn_kernel(x_ref, gpre_ref, gpost_ref, win_ref, cw_ref, cb_ref, wout_ref, o_ref,
                stage_in, stage_out, h_scr, act_scr, halo, *, tm, fc, nc):
    F = nc * fc
    D = wout_ref.shape[1]
    sub_rows = tm // FFN_SUBTILES
    Q = sub_rows // SUBLANES
    P = Q + STAGE_PAD
    nslab = D // LANES

    @pl.when(pl.program_id(1) == 0)
    def _():
        halo[...] = jnp.zeros(halo.shape, F32)

    sub = lax.broadcasted_iota(jnp.int32, (SUBLANES, fc), 0)

    def rows(t):
        return slice(t * sub_rows, (t + 1) * sub_rows)

    def pre_norm(t):
        h = _rms(x_ref[0, rows(t), :], gpre_ref[...])
        for j in range(nslab):
            for i in range(SUBLANES):
                stage_in[t, j, i * P:i * P + Q, :] = h[i * Q:(i + 1) * Q, j * LANES:(j + 1) * LANES]
        for k in range(0, Q, 2):
            blk = [jnp.concatenate([stage_in[t, j, pl.ds(k + d, SUBLANES, stride=P), :] for j in range(nslab)], axis=1)
                   for d in range(2)]
            h_scr[t, k * SUBLANES:(k + 2) * SUBLANES, :] = jnp.concatenate(blk, axis=0).astype(BF16)

    def conv(t, col):
        u = jnp.dot(h_scr[t], win_ref[:, col:col + fc], preferred_element_type=F32)
        last1, last2 = u[sub_rows - SUBLANES:, :], u[sub_rows - 2 * SUBLANES:sub_rows - SUBLANES, :]
        b1 = pltpu.roll(jnp.where(sub == SUBLANES - 1, halo[0:SUBLANES, col:col + fc], last1), 1, 0)
        b2 = pltpu.roll(jnp.where(sub == SUBLANES - 1, halo[SUBLANES:2 * SUBLANES, col:col + fc], last2), 1, 0)
        halo[0:SUBLANES, col:col + fc] = last1
        halo[SUBLANES:2 * SUBLANES, col:col + fc] = last2
        um1 = jnp.concatenate([b1, u[:sub_rows - SUBLANES, :]], axis=0)
        um2 = jnp.concatenate([b2, b1, u[:sub_rows - 2 * SUBLANES, :]], axis=0)
        return (cw_ref[0:1, col:col + fc] * um2 + cw_ref[1:2, col:col + fc] * um1
                + cw_ref[2:3, col:col + fc] * u + cb_ref[:, col:col + fc])

    def up(t):
        for c in range(nc):
            gate = conv(t, c * fc)
            val = conv(t, F + c * fc)
            act_scr[t, :, c * fc:(c + 1) * fc] = _gelu_gate(gate, val).astype(BF16)

    def down(t):
        f = jnp.dot(act_scr[t], wout_ref[...], preferred_element_type=F32)
        for k in range(Q):
            for j in range(nslab):
                stage_out[t, j, pl.ds(k, SUBLANES, stride=P), :] = f[k * SUBLANES:(k + 1) * SUBLANES, j * LANES:(j + 1) * LANES]

    def post_norm(t):
        fn = jnp.concatenate([jnp.concatenate([stage_out[t, j, i * P:i * P + Q, :] for i in range(SUBLANES)], axis=0)
                              for j in range(nslab)], axis=1)
        o_ref[0, rows(t), :] = x_ref[0, rows(t), :] + _rms(fn, gpost_ref[...])

    pre_norm(0)
    for t in range(FFN_SUBTILES):
        up(t)
        if t + 1 < FFN_SUBTILES:
            pre_norm(t + 1)
        if t >= 1:
            post_norm(t - 1)
        down(t)
    post_norm(FFN_SUBTILES - 1)


def _ffn_layer(x, gpre, gpost, win, cw, cb, wout, *, tm):
    B, S, D = x.shape
    F = wout.shape[0]
    fc = FFN_CHUNK
    nc = F // fc
    sub_rows = tm // FFN_SUBTILES
    assert cw.shape == (CONV_WIDTH, 2 * F) and sub_rows % (2 * SUBLANES) == 0
    xspec = pl.BlockSpec((1, tm, D), lambda b, s: (b, s, 0))
    stage = pltpu.VMEM((FFN_SUBTILES, D // LANES, sub_rows + SUBLANES * STAGE_PAD, LANES), F32)
    return pl.pallas_call(
        functools.partial(_ffn_kernel, tm=tm, fc=fc, nc=nc),
        grid=(B, S // tm),
        in_specs=[xspec, _const_spec((1, D)), _const_spec((1, D)), _const_spec(win.shape),
                  _const_spec(cw.shape), _const_spec(cb.shape), _const_spec(wout.shape)],
        out_specs=xspec,
        out_shape=jax.ShapeDtypeStruct(x.shape, x.dtype),
        scratch_shapes=[stage, stage,
                        pltpu.VMEM((FFN_SUBTILES, sub_rows, D), BF16), pltpu.VMEM((FFN_SUBTILES, sub_rows, F), BF16),
                        pltpu.VMEM(((CONV_WIDTH - 1) * SUBLANES, 2 * F), F32)],
        compiler_params=_params(),
        name="ffn_layer",
    )(x, gpre, gpost, win, cw, cb, wout)


def _rope_kernel(pos_ref, invf_ref, sign_ref, cos_ref, sin_ref, *, tm):
    ngrp = LANES // HALF
    q4 = tm // ngrp
    grp = lax.broadcasted_iota(jnp.int32, (q4, LANES), 1) // HALF
    pos = pos_ref[0].astype(F32)
    packed = jnp.zeros((q4, LANES), F32)
    for g in range(ngrp):
        packed = jnp.where(grp == g, pos[g * q4:(g + 1) * q4, :], packed)
    ang = packed * invf_ref[...]
    for table, ref, scale in ((jnp.cos(ang), cos_ref, None), (jnp.sin(ang), sin_ref, sign_ref[...])):
        rolled = [table] + [pltpu.roll(table, HALF * d, 1) for d in range(1, ngrp)]
        for g in range(ngrp):
            out = rolled[(0 - g) % ngrp]
            for h in range(1, ngrp):
                out = jnp.where(grp == h, rolled[(h - g) % ngrp], out)
            ref[0, g * q4:(g + 1) * q4, :] = out if scale is None else out * scale


def _rope_tables(positions, *, tm):
    B, S = positions.shape
    inv_freq = 1.0 / (ROPE_THETA ** (jnp.arange(0, HEAD_DIM, 2, dtype=F32) / HEAD_DIM))
    invf = jnp.tile(inv_freq, LANES // HALF)[None, :]
    sign = jnp.where(jnp.arange(LANES) < LANES // 2, -1.0, 1.0).astype(F32)[None, :]
    tspec = pl.BlockSpec((1, tm, LANES), lambda b, s: (b, s, 0))
    return pl.pallas_call(
        functools.partial(_rope_kernel, tm=tm),
        grid=(B, S // tm),
        in_specs=[pl.BlockSpec((1, tm, 1), lambda b, s: (b, s, 0)),
                  _const_spec((1, LANES)), _const_spec((1, LANES))],
        out_specs=[tspec, tspec],
        out_shape=[jax.ShapeDtypeStruct((B, S, LANES), F32)] * 2,
        compiler_params=_params(),
        name="rope_tables",
    )(positions[:, :, None], invf, sign)


def _rope(t, cos, sin):
    return t * cos + pltpu.roll(t, LANES // 2, 1) * sin


def _kv_kernel(x_ref, g_ref, cos_ref, sin_ref, wk_ref, wv_ref, k_ref, v_ref):
    h = _rms(x_ref[0], g_ref[...]).astype(BF16)
    kd = jnp.dot(h, wk_ref[...], preferred_element_type=F32)
    cos, sin = cos_ref[0], sin_ref[0]
    for j in range(N_KV_HEADS):
        k_ref[0, :, j * LANES:(j + 1) * LANES] = _rope(kd[:, j * LANES:(j + 1) * LANES], cos, sin).astype(BF16)
    v_ref[0] = jnp.dot(h, wv_ref[...], preferred_element_type=F32).astype(BF16)


def _kv_proj(x, g, cos, sin, wk, wv, *, tm):
    B, S, D = x.shape
    W = N_KV_HEADS * LANES
    tspec = pl.BlockSpec((1, tm, LANES), lambda b, s: (b, s, 0))
    ospec = pl.BlockSpec((1, tm, W), lambda b, s: (b, s, 0))
    return pl.pallas_call(
        _kv_kernel,
        grid=(B, S // tm),
        in_specs=[pl.BlockSpec((1, tm, D), lambda b, s: (b, s, 0)), _const_spec((1, D)), tspec, tspec,
                  _const_spec(wk.shape), _const_spec(wv.shape)],
        out_specs=[ospec, ospec],
        out_shape=[jax.ShapeDtypeStruct((B, S, W), BF16)] * 2,
        compiler_params=_params(),
        name="kv_proj",
    )(x, g, cos, sin, wk, wv)


def _attn_kernel(sinks_ref, x_ref, gpre_ref, gpost_ref, cos_ref, sin_ref, kc_ref, kp_ref, vc_ref, vp_ref,
                 wq_ref, wo_ref, o_ref, qst, kbuf, vbuf, o_scr, *, tm):
    s_idx = pl.program_id(1)
    x = x_ref[0]
    h = _rms(x, gpre_ref[...]).astype(BF16)
    q = jnp.dot(h, wq_ref[...], preferred_element_type=F32)
    cos, sin = cos_ref[0], sin_ref[0]
    lane = lax.broadcasted_iota(jnp.int32, (1, LANES), 1)
    first_of_pair = (lane % HEAD_DIM) < HALF
    for p in range(N_Q_HEADS // 2):
        qr = _rope(q[:, p * LANES:(p + 1) * LANES], cos, sin) * (ATTN_SCALE * LOG2E)
        qst[2 * p] = jnp.where(first_of_pair, qr, 0.0).astype(BF16)
        qst[2 * p + 1] = jnp.where(first_of_pair, 0.0, qr).astype(BF16)

    kbuf[0:BLOCK, :] = kp_ref[0]
    kbuf[BLOCK:BLOCK + tm, :] = kc_ref[0]
    for j in range(N_KV_HEADS):
        vbuf[0:BLOCK, 2 * j * LANES:(2 * j + 1) * LANES] = vp_ref[0, :, j * LANES:(j + 1) * LANES]
        vbuf[BLOCK:BLOCK + tm, 2 * j * LANES:(2 * j + 1) * LANES] = vc_ref[0, :, j * LANES:(j + 1) * LANES]
        vbuf[:, (2 * j + 1) * LANES:(2 * j + 2) * LANES] = jnp.ones((tm + BLOCK, LANES), BF16)

    qi = lax.broadcasted_iota(jnp.int32, (BLOCK, 2 * BLOCK), 0)
    kj = lax.broadcasted_iota(jnp.int32, (BLOCK, 2 * BLOCK), 1)
    rel = BLOCK + qi - kj
    band = (rel >= 0) & (rel < WINDOW)
    out_lo = lax.broadcasted_iota(jnp.int32, (BLOCK, LANES), 1) < HEAD_DIM

    def block_body(jb, carry):
        r0 = pl.multiple_of(jb * BLOCK, BLOCK)
        seq_start = (s_idx * tm + r0 == 0).astype(jnp.int32)
        valid = band & (kj >= BLOCK * seq_start)
        for j in range(N_KV_HEADS):
            qblk = jnp.concatenate([qst[Q_PER_KV * j + i, pl.ds(r0, BLOCK), :] for i in range(Q_PER_KV)], axis=0)
            kblk = kbuf[pl.ds(r0, 2 * BLOCK), j * LANES:(j + 1) * LANES]
            sc = lax.dot_general(qblk, kblk, (((1,), (1,)), ((), ())), preferred_element_type=F32)
            ps, sink_p = [], []
            for i in range(Q_PER_KV):
                sink = sinks_ref[Q_PER_KV * j + i] * LOG2E
                sh = jnp.where(valid, sc[i * BLOCK:(i + 1) * BLOCK], NEG_INF)
                m = jnp.maximum(jnp.max(sh, axis=-1, keepdims=True), sink)
                ps.append(jnp.exp2(sh - m).astype(BF16))
                sink_p.append(jnp.exp2(sink - m))
            vblk = vbuf[pl.ds(r0, 2 * BLOCK), 2 * j * LANES:(2 * j + 2) * LANES]
            ov = jnp.dot(jnp.concatenate(ps, axis=0), vblk, preferred_element_type=F32)
            heads = []
            for i in range(Q_PER_KV):
                num = ov[i * BLOCK:(i + 1) * BLOCK, :LANES]
                den = ov[i * BLOCK:(i + 1) * BLOCK, LANES:] + sink_p[i]
                heads.append(num / den)
            for pp in range(Q_PER_KV // 2):
                col = (Q_PER_KV // 2 * j + pp) * LANES
                o_scr[pl.ds(r0, BLOCK), col:col + LANES] = jnp.where(out_lo, heads[2 * pp], heads[2 * pp + 1]).astype(BF16)
        return carry

    lax.fori_loop(0, tm // BLOCK, block_body, 0)
    mo = jnp.dot(o_scr[...], wo_ref[...], preferred_element_type=F32)
    o_ref[0] = x + _rms(mo, gpost_ref[...])


def _attn_layer(x, gpre, gpost, cos, sin, k, v, wq, wo, sinks, *, tm):
    B, S, D = x.shape
    W = N_KV_HEADS * LANES
    nbt = tm // BLOCK
    xspec = pl.BlockSpec((1, tm, D), lambda b, s: (b, s, 0))
    tspec = pl.BlockSpec((1, tm, LANES), lambda b, s: (b, s, 0))
    cur = pl.BlockSpec((1, tm, W), lambda b, s: (b, s, 0))
    prev = pl.BlockSpec((1, BLOCK, W), lambda b, s: (b, jnp.maximum(s * nbt - 1, 0), 0))
    return pl.pallas_call(
        functools.partial(_attn_kernel, tm=tm),
        grid=(B, S // tm),
        in_specs=[pl.BlockSpec(memory_space=pltpu.SMEM), xspec, _const_spec((1, D)), _const_spec((1, D)),
                  tspec, tspec, cur, prev, cur, prev, _const_spec(wq.shape), _const_spec(wo.shape)],
        out_specs=xspec,
        out_shape=jax.ShapeDtypeStruct(x.shape, x.dtype),
        scratch_shapes=[pltpu.VMEM((N_Q_HEADS, tm, LANES), BF16),
                        pltpu.VMEM((tm + BLOCK, W), BF16), pltpu.VMEM((tm + BLOCK, 2 * W), BF16),
                        pltpu.VMEM((tm, D), BF16)],
        compiler_params=_params(),
        name="attn_layer",
    )(sinks, x, gpre, gpost, cos, sin, k, k, v, v, wq, wo)


def _pair_layout_q(wq):
    D = wq.shape[0]
    w = wq.reshape(D, N_Q_HEADS // 2, 2, 2, HALF)
    w = jnp.swapaxes(w, 2, 3)
    return w.reshape(D, N_Q_HEADS * HEAD_DIM)


def _dup_layout_k(wk):
    D = wk.shape[0]
    w = wk.reshape(D, N_KV_HEADS, 2, 1, HALF)
    w = jnp.broadcast_to(w, (D, N_KV_HEADS, 2, 2, HALF))
    return w.reshape(D, N_KV_HEADS * LANES)


def _dup_layout_v(wv):
    D = wv.shape[0]
    w = wv.reshape(D, N_KV_HEADS, 1, HEAD_DIM)
    w = jnp.broadcast_to(w, (D, N_KV_HEADS, 2, HEAD_DIM))
    return w.reshape(D, N_KV_HEADS * LANES)


def kernel(x, positions, mix_pre_g, mix_post_g, pool_w, pool_scale, kv_norm_g, w_kv, w_q, w_o, sinks,
           ffn_pre_g, ffn_post_g, ffn_w_in, ffn_conv_w, ffn_conv_b, ffn_w_out):
    B, S, D = x.shape
    tm = min(TILE_ROWS, S)
    tm_ffn = min(FFN_TILE_ROWS, S)
    depth = mix_pre_g.shape[0]
    n_a = pool_w.shape[0]
    F = ffn_w_out.shape[1]
    hkvd = N_KV_HEADS * HEAD_DIM
    assert S % tm == 0 and S % tm_ffn == 0 and tm % BLOCK == 0 and F % FFN_CHUNK == 0

    cos = sin = k = v = None
    for layer in range(depth):
        gpre, gpost = mix_pre_g[layer][None, :], mix_post_g[layer][None, :]
        if layer < n_a:
            x = _pool_layer(x, gpre, gpost, pool_w[layer].astype(BF16), pool_scale[layer][None, :], tm=tm)
        else:
            if layer == n_a:
                cos, sin = _rope_tables(positions, tm=tm)
                k, v = _kv_proj(x, kv_norm_g[None, :], cos, sin,
                                _dup_layout_k(w_kv[:, :hkvd]).astype(BF16),
                                _dup_layout_v(w_kv[:, hkvd:]).astype(BF16), tm=tm)
            j = layer - n_a
            x = _attn_layer(x, gpre, gpost, cos, sin, k, v, _pair_layout_q(w_q[j]).astype(BF16),
                            w_o[j].astype(BF16), sinks[j], tm=tm)
        x = _ffn_layer(x, ffn_pre_g[layer][None, :], ffn_post_g[layer][None, :],
                       ffn_w_in[layer].astype(BF16), ffn_conv_w[layer], ffn_conv_b[layer][None, :],
                       ffn_w_out[layer].astype(BF16), tm=tm_ffn)
    return x
```

```python
import functools
import math

import jax
import jax.numpy as jnp
from jax import lax
from jax.experimental import pallas as pl
from jax.experimental.pallas import tpu as pltpu

F32 = jnp.float32
BF16 = jnp.bfloat16

RMS_EPS = 1e-6
POOL_WINDOWS = (2, 4, 8, 16)
HEAD_DIM = 64
HALF = HEAD_DIM // 2
N_Q_HEADS = 16
N_KV_HEADS = 4
Q_PER_KV = N_Q_HEADS // N_KV_HEADS
WINDOW = 128
BLOCK = 128
ROPE_THETA = 10000.0
ATTN_SCALE = 1.0 / math.sqrt(HEAD_DIM)
NEG_INF = -1e30
CONV_WIDTH = 3
LANES = 128
SUBLANES = 8
STAGE_PAD = 8
FFN_CHUNK = 256
FFN_SUBTILES = 2
VMEM_LIMIT = 56 * 1024 * 1024

TILE_ROWS = 1024
FFN_TILE_ROWS = 1024
ATTN_TILE_ROWS = 1024

GELU_C = math.sqrt(2.0 / math.pi)
LOG2E = math.log2(math.e)


def _rms(x, g):
    ms = jnp.mean(x * x, axis=-1, keepdims=True)
    return x * lax.rsqrt(ms + RMS_EPS) * g


def _const_spec(shape):
    nd = len(shape)
    return pl.BlockSpec(shape, lambda b, s: (0,) * nd, pipeline_mode=pl.Buffered(1))


def _params():
    return pltpu.CompilerParams(dimension_semantics=("arbitrary", "arbitrary"),
                                vmem_limit_bytes=VMEM_LIMIT)


def _pool_kernel(x_ref, gpre_ref, gpost_ref, wp_ref, scale_ref, o_ref, stage, hp, hprev, *, tm, gc):
    s_idx = pl.program_id(1)
    D = hp.shape[1]
    Q = tm // SUBLANES
    P = Q + STAGE_PAD
    nslab = D // LANES
    wmax = max(POOL_WINDOWS)
    tail = slice(tm - wmax * SUBLANES, tm)

    @pl.when(s_idx == 0)
    def _():
        hprev[...] = jnp.zeros(hprev.shape, F32)

    x = x_ref[0]
    h = _rms(x, gpre_ref[...])
    for j in range(nslab):
        for i in range(SUBLANES):
            stage[j, i * P:i * P + Q, :] = h[i * Q:(i + 1) * Q, j * LANES:(j + 1) * LANES]
    for k in range(Q):
        hp[k * SUBLANES:(k + 1) * SUBLANES, :] = jnp.concatenate(
            [stage[j, pl.ds(k, SUBLANES, stride=P), :] for j in range(nslab)], axis=1)

    sub = lax.broadcasted_iota(jnp.int32, (wmax * SUBLANES, D), 0) % SUBLANES
    wrap = jnp.where(sub == SUBLANES - 1, hprev[...], hp[tail, :])
    wrap = jnp.concatenate([pltpu.roll(wrap[m * SUBLANES:(m + 1) * SUBLANES, :], 1, 0) for m in range(wmax)], axis=0)
    hprev[...] = hp[tail, :]

    first = (s_idx == 0).astype(F32)
    sub8 = lax.broadcasted_iota(jnp.int32, (SUBLANES, gc), 0)
    ys = []
    for gi, w in enumerate(POOL_WINDOWS):
        cols = slice(gi * gc, (gi + 1) * gc)
        own = hp[:, cols]
        acc = own
        for sft in range(1, w):
            acc = acc + jnp.concatenate([wrap[(wmax - sft) * SUBLANES:, cols], hp[0:tm - sft * SUBLANES, cols]], axis=0)
        mean = acc * (1.0 / w)
        fix = [jnp.where(sub8 == 0, 1.0 + first * (w / (k + 1.0) - 1.0), 1.0) for k in range(w - 1)]
        head = jnp.concatenate(fix, axis=0) * mean[0:(w - 1) * SUBLANES, :]
        mean = jnp.concatenate([head, mean[(w - 1) * SUBLANES:, :]], axis=0)
        d = mean - own
        ys.append(jnp.dot(d.astype(BF16), wp_ref[gi], preferred_element_type=F32))
    m = jnp.concatenate(ys, axis=-1) * scale_ref[...]
    r = _rms(m, gpost_ref[...])
    for k in range(Q):
        for j in range(nslab):
            stage[j, pl.ds(k, SUBLANES, stride=P), :] = r[k * SUBLANES:(k + 1) * SUBLANES, j * LANES:(j + 1) * LANES]
    rn = jnp.concatenate([jnp.concatenate([stage[j, i * P:i * P + Q, :] for i in range(SUBLANES)], axis=0)
                          for j in range(nslab)], axis=1)
    o_ref[0] = x + rn


def _pool_layer(x, gpre, gpost, wp, scale, *, tm):
    B, S, D = x.shape
    gc = D // len(POOL_WINDOWS)
    wmax = max(POOL_WINDOWS)
    assert tm // SUBLANES >= wmax
    xspec = pl.BlockSpec((1, tm, D), lambda b, s: (b, s, 0))
    return pl.pallas_call(
        functools.partial(_pool_kernel, tm=tm, gc=gc),
        grid=(B, S // tm),
        in_specs=[xspec, _const_spec((1, D)), _const_spec((1, D)),
                  _const_spec(wp.shape), _const_spec((1, D))],
        out_specs=xspec,
        out_shape=jax.ShapeDtypeStruct(x.shape, x.dtype),
        scratch_shapes=[pltpu.VMEM((D // LANES, tm + SUBLANES * STAGE_PAD, LANES), F32),
                        pltpu.VMEM((tm, D), F32), pltpu.VMEM((wmax * SUBLANES, D), F32)],
        compiler_params=_params(),
        name="pool_layer",
    )(x, gpre, gpost, wp, scale)


def _gelu_gate(gate, val):
    inner = gate * (GELU_C + (GELU_C * 0.044715) * (gate * gate))
    return (gate * val) * (0.5 * jnp.tanh(inner) + 0.5)


def _ffn_kernel(x_ref, gpre_ref, gpost_ref, win_ref, cw_ref, cb_ref, wout_ref, o_ref,
                stage_in, stage_out, h_scr, act_scr, halo, *, tm, fc, nc):
    F = nc * fc
    D = wout_ref.shape[1]
    sub_rows = tm // FFN_SUBTILES
    Q = sub_rows // SUBLANES
    P = Q + STAGE_PAD
    nslab = D // LANES

    @pl.when(pl.program_id(1) == 0)
    def _():
        halo[...] = jnp.zeros(halo.shape, F32)

    sub = lax.broadcasted_iota(jnp.int32, (SUBLANES, fc), 0)

    def rows(t):
        return slice(t * sub_rows, (t + 1) * sub_rows)

    def pre_norm(t):
        h = _rms(x_ref[0, rows(t), :], gpre_ref[...])
        for j in range(nslab):
            for i in range(SUBLANES):
                stage_in[t, j, i * P:i * P + Q, :] = h[i * Q:(i + 1) * Q, j * LANES:(j + 1) * LANES]
        for k in range(0, Q, 2):
            blk = [jnp.concatenate([stage_in[t, j, pl.ds(k + d, SUBLANES, stride=P), :] for j in range(nslab)], axis=1)
                   for d in range(2)]
            h_scr[t, k * SUBLANES:(k + 2) * SUBLANES, :] = jnp.concatenate(blk, axis=0).astype(BF16)

    def conv(t, col):
        u = jnp.dot(h_scr[t], win_ref[:, col:col + fc], preferred_element_type=F32)
        last1, last2 = u[sub_rows - SUBLANES:, :], u[sub_rows - 2 * SUBLANES:sub_rows - SUBLANES, :]
        b1 = pltpu.roll(jnp.where(sub == SUBLANES - 1, halo[0:SUBLANES, col:col + fc], last1), 1, 0)
        b2 = pltpu.roll(jnp.where(sub == SUBLANES - 1, halo[SUBLANES:2 * SUBLANES, col:col + fc], last2), 1, 0)
        halo[0:SUBLANES, col:col + fc] = last1
        halo[SUBLANES:2 * SUBLANES, col:col + fc] = last2
        um1 = jnp.concatenate([b1, u[:sub_rows - SUBLANES, :]], axis=0)
        um2 = jnp.concatenate([b2, b1, u[:sub_rows - 2 * SUBLANES, :]], axis=0)
        return (cw_ref[0:1, col:col + fc] * um2 + cw_ref[1:2, col:col + fc] * um1
                + cw_ref[2:3, col:col + fc] * u + cb_ref[:, col:col + fc])

    def up(t):
        for c in range(nc):
            gate = conv(t, c * fc)
            val = conv(t, F + c * fc)
            act_scr[t, :, c * fc:(c + 1) * fc] = _gelu_gate(gate, val).astype(BF16)

    def down(t):
        f = jnp.dot(act_scr[t], wout_ref[...], preferred_element_type=F32)
        for k in range(Q):
            for j in range(nslab):
                stage_out[t, j, pl.ds(k, SUBLANES, stride=P), :] = f[k * SUBLANES:(k + 1) * SUBLANES, j * LANES:(j + 1) * LANES]

    def post_norm(t):
        fn = jnp.concatenate([jnp.concatenate([stage_out[t, j, i * P:i * P + Q, :] for i in range(SUBLANES)], axis=0)
                              for j in range(nslab)], axis=1)
        o_ref[0, rows(t), :] = x_ref[0, rows(t), :] + _rms(fn, gpost_ref[...])

    pre_norm(0)
    for t in range(FFN_SUBTILES):
        up(t)
        if t + 1 < FFN_SUBTILES:
            pre_norm(t + 1)
        if t >= 1:
            post_norm(t - 1)
        down(t)
    post_norm(FFN_SUBTILES - 1)


def _ffn_layer(x, gpre, gpost, win, cw, cb, wout, *, tm):
    B, S, D = x.shape
    F = wout.shape[0]
    fc = FFN_CHUNK
    nc = F // fc
    sub_rows = tm // FFN_SUBTILES
    assert cw.shape == (CONV_WIDTH, 2 * F) and sub_rows % (2 * SUBLANES) == 0
    xspec = pl.BlockSpec((1, tm, D), lambda b, s: (b, s, 0))
    stage = pltpu.VMEM((FFN_SUBTILES, D // LANES, sub_rows + SUBLANES * STAGE_PAD, LANES), F32)
    return pl.pallas_call(
        functools.partial(_ffn_kernel, tm=tm, fc=fc, nc=nc),
        grid=(B, S // tm),
        in_specs=[xspec, _const_spec((1, D)), _const_spec((1, D)), _const_spec(win.shape),
                  _const_spec(cw.shape), _const_spec(cb.shape), _const_spec(wout.shape)],
        out_specs=xspec,
        out_shape=jax.ShapeDtypeStruct(x.shape, x.dtype),
        scratch_shapes=[stage, stage,
                        pltpu.VMEM((FFN_SUBTILES, sub_rows, D), BF16), pltpu.VMEM((FFN_SUBTILES, sub_rows, F), BF16),
                        pltpu.VMEM(((CONV_WIDTH - 1) * SUBLANES, 2 * F), F32)],
        compiler_params=_params(),
        name="ffn_layer",
    )(x, gpre, gpost, win, cw, cb, wout)


def _rope_kernel(pos_ref, invf_ref, sign_ref, cos_ref, sin_ref, *, tm):
    ngrp = LANES // HALF
    q4 = tm // ngrp
    grp = lax.broadcasted_iota(jnp.int32, (q4, LANES), 1) // HALF
    pos = pos_ref[0].astype(F32)
    packed = jnp.zeros((q4, LANES), F32)
    for g in range(ngrp):
        packed = jnp.where(grp == g, pos[g * q4:(g + 1) * q4, :], packed)
    ang = packed * invf_ref[...]
    for table, ref, scale in ((jnp.cos(ang), cos_ref, None), (jnp.sin(ang), sin_ref, sign_ref[...])):
        rolled = [table] + [pltpu.roll(table, HALF * d, 1) for d in range(1, ngrp)]
        for g in range(ngrp):
            out = rolled[(0 - g) % ngrp]
            for h in range(1, ngrp):
                out = jnp.where(grp == h, rolled[(h - g) % ngrp], out)
            ref[0, g * q4:(g + 1) * q4, :] = out if scale is None else out * scale


def _rope_tables(positions, *, tm):
    B, S = positions.shape
    inv_freq = 1.0 / (ROPE_THETA ** (jnp.arange(0, HEAD_DIM, 2, dtype=F32) / HEAD_DIM))
    invf = jnp.tile(inv_freq, LANES // HALF)[None, :]
    sign = jnp.where(jnp.arange(LANES) < LANES // 2, -1.0, 1.0).astype(F32)[None, :]
    tspec = pl.BlockSpec((1, tm, LANES), lambda b, s: (b, s, 0))
    return pl.pallas_call(
        functools.partial(_rope_kernel, tm=tm),
        grid=(B, S // tm),
        in_specs=[pl.BlockSpec((1, tm, 1), lambda b, s: (b, s, 0)),
                  _const_spec((1, LANES)), _const_spec((1, LANES))],
        out_specs=[tspec, tspec],
        out_shape=[jax.ShapeDtypeStruct((B, S, LANES), F32)] * 2,
        compiler_params=_params(),
        name="rope_tables",
    )(positions[:, :, None], invf, sign)


def _rope(t, cos, sin):
    return t * cos + pltpu.roll(t, LANES // 2, 1) * sin


def _kv_kernel(x_ref, g_ref, cos_ref, sin_ref, wk_ref, wv_ref, k_ref, v_ref):
    h = _rms(x_ref[0], g_ref[...]).astype(BF16)
    kd = jnp.dot(h, wk_ref[...], preferred_element_type=F32)
    cos, sin = cos_ref[0], sin_ref[0]
    for j in range(N_KV_HEADS):
        k_ref[0, :, j * LANES:(j + 1) * LANES] = _rope(kd[:, j * LANES:(j + 1) * LANES], cos, sin).astype(BF16)
    v_ref[0] = jnp.dot(h, wv_ref[...], preferred_element_type=F32).astype(BF16)


def _kv_proj(x, g, cos, sin, wk, wv, *, tm):
    B, S, D = x.shape
    W = N_KV_HEADS * LANES
    tspec = pl.BlockSpec((1, tm, LANES), lambda b, s: (b, s, 0))
    ospec = pl.BlockSpec((1, tm, W), lambda b, s: (b, s, 0))
    return pl.pallas_call(
        _kv_kernel,
        grid=(B, S // tm),
        in_specs=[pl.BlockSpec((1, tm, D), lambda b, s: (b, s, 0)), _const_spec((1, D)), tspec, tspec,
                  _const_spec(wk.shape), _const_spec(wv.shape)],
        out_specs=[ospec, ospec],
        out_shape=[jax.ShapeDtypeStruct((B, S, W), BF16)] * 2,
        compiler_params=_params(),
        name="kv_proj",
    )(x, g, cos, sin, wk, wv)


def _attn_kernel(sinks_ref, x_ref, gpre_ref, gpost_ref, cos_ref, sin_ref, kc_ref, kp_ref, vc_ref, vp_ref,
                 wq_ref, wo_ref, o_ref, qa, qb, kbuf, vbuf, o_scr, *, tm):
    s_idx = pl.program_id(1)
    nblk = tm // BLOCK
    D = wq_ref.shape[0]
    cw = D // N_KV_HEADS
    lane = lax.broadcasted_iota(jnp.int32, (1, LANES), 1)
    first_of_pair = (lane % HEAD_DIM) < HALF

    def pre_norm(r0):
        return _rms(x_ref[0, pl.ds(r0, BLOCK), :], gpre_ref[...]).astype(BF16)

    def q_piece(h, j, r0, qdst):
        q = jnp.dot(h, wq_ref[:, j * cw:(j + 1) * cw], preferred_element_type=F32)
        cos, sin = cos_ref[0, pl.ds(r0, BLOCK), :], sin_ref[0, pl.ds(r0, BLOCK), :]
        for p in range(cw // LANES):
            qr = _rope(q[:, p * LANES:(p + 1) * LANES], cos, sin) * (ATTN_SCALE * LOG2E)
            qdst[Q_PER_KV * j + 2 * p] = jnp.where(first_of_pair, qr, 0.0).astype(BF16)
            qdst[Q_PER_KV * j + 2 * p + 1] = jnp.where(first_of_pair, 0.0, qr).astype(BF16)

    def post_norm(mo_parts, r0):
        mo = jnp.concatenate(mo_parts, axis=-1)
        o_ref[0, pl.ds(r0, BLOCK), :] = x_ref[0, pl.ds(r0, BLOCK), :] + _rms(mo, gpost_ref[...])

    kbuf[0:BLOCK, :] = kp_ref[0]
    kbuf[BLOCK:BLOCK + tm, :] = kc_ref[0]
    for j in range(N_KV_HEADS):
        vbuf[0:BLOCK, 2 * j * LANES:(2 * j + 1) * LANES] = vp_ref[0, :, j * LANES:(j + 1) * LANES]
        vbuf[BLOCK:BLOCK + tm, 2 * j * LANES:(2 * j + 1) * LANES] = vc_ref[0, :, j * LANES:(j + 1) * LANES]
        vbuf[:, (2 * j + 1) * LANES:(2 * j + 2) * LANES] = jnp.ones((tm + BLOCK, LANES), BF16)

    qi = lax.broadcasted_iota(jnp.int32, (BLOCK, 2 * BLOCK), 0)
    kj = lax.broadcasted_iota(jnp.int32, (BLOCK, 2 * BLOCK), 1)
    rel = BLOCK + qi - kj
    band = (rel >= 0) & (rel < WINDOW)
    out_lo = lax.broadcasted_iota(jnp.int32, (BLOCK, LANES), 1) < HEAD_DIM

    def block_step(jb, qcur, qnext, has_prev=True, has_next=True):
        r0 = pl.multiple_of(jb * BLOCK, BLOCK)
        rp = pl.multiple_of((jb - 1) * BLOCK, BLOCK) if has_prev else None
        rn = pl.multiple_of((jb + 1) * BLOCK, BLOCK) if has_next else None
        o_prev = o_scr[pl.ds(rp, BLOCK), :] if has_prev else None
        h_next = pre_norm(rn) if has_next else None
        seq_start = (s_idx * tm + r0 == 0).astype(jnp.int32)
        valid = band & (kj >= BLOCK * seq_start)
        mo_parts = []

        def scores(j):
            qblk = jnp.concatenate([qcur[Q_PER_KV * j + i] for i in range(Q_PER_KV)], axis=0)
            kblk = kbuf[pl.ds(r0, 2 * BLOCK), j * LANES:(j + 1) * LANES]
            return lax.dot_general(qblk, kblk, (((1,), (1,)), ((), ())), preferred_element_type=F32)

        def projections(j):
            if has_prev:
                mo_parts.append(jnp.dot(o_prev, wo_ref[:, j * cw:(j + 1) * cw], preferred_element_type=F32))
            if has_next:
                q_piece(h_next, j, rn, qnext)

        def values(j, sc):
            ps, sink_p = [], []
            for i in range(Q_PER_KV):
                sink = sinks_ref[Q_PER_KV * j + i] * LOG2E
                sh = jnp.where(valid, sc[i * BLOCK:(i + 1) * BLOCK], NEG_INF)
                m = jnp.maximum(jnp.max(sh, axis=-1, keepdims=True), sink)
                ps.append(jnp.exp2(sh - m).astype(BF16))
                sink_p.append(jnp.exp2(sink - m))
            vblk = vbuf[pl.ds(r0, 2 * BLOCK), 2 * j * LANES:(2 * j + 2) * LANES]
            ov = jnp.dot(jnp.concatenate(ps, axis=0), vblk, preferred_element_type=F32)
            heads = []
            for i in range(Q_PER_KV):
                num = ov[i * BLOCK:(i + 1) * BLOCK, :LANES]
                den = ov[i * BLOCK:(i + 1) * BLOCK, LANES:] + sink_p[i]
                heads.append(num / den)
            for pp in range(Q_PER_KV // 2):
                col = (Q_PER_KV // 2 * j + pp) * LANES
                o_scr[pl.ds(r0, BLOCK), col:col + LANES] = jnp.where(out_lo, heads[2 * pp], heads[2 * pp + 1]).astype(BF16)

        sc = scores(0)
        for j in range(N_KV_HEADS):
            sc_next = scores(j + 1) if j + 1 < N_KV_HEADS else None
            projections(j)
            values(j, sc)
            sc = sc_next
        if has_prev:
            post_norm(mo_parts, rp)

    h0 = pre_norm(0)
    for j in range(N_KV_HEADS):
        q_piece(h0, j, 0, qa)

    block_step(0, qa, qb, has_prev=False)

    def pair_body(i, carry):
        block_step(2 * i + 1, qb, qa)
        block_step(2 * i + 2, qa, qb)
        return carry

    lax.fori_loop(0, nblk // 2 - 1, pair_body, 0)
    block_step(nblk - 1, qb, qa, has_next=False)
    r_last = (nblk - 1) * BLOCK
    mo_last = jnp.dot(o_scr[r_last:r_last + BLOCK, :], wo_ref[...], preferred_element_type=F32)
    o_ref[0, r_last:r_last + BLOCK, :] = x_ref[0, r_last:r_last + BLOCK, :] + _rms(mo_last, gpost_ref[...])


def _attn_layer(x, gpre, gpost, cos, sin, k, v, wq, wo, sinks, *, tm):
    B, S, D = x.shape
    W = N_KV_HEADS * LANES
    nbt = tm // BLOCK
    xspec = pl.BlockSpec((1, tm, D), lambda b, s: (b, s, 0))
    tspec = pl.BlockSpec((1, tm, LANES), lambda b, s: (b, s, 0))
    cur = pl.BlockSpec((1, tm, W), lambda b, s: (b, s, 0))
    prev = pl.BlockSpec((1, BLOCK, W), lambda b, s: (b, jnp.maximum(s * nbt - 1, 0), 0))
    return pl.pallas_call(
        functools.partial(_attn_kernel, tm=tm),
        grid=(B, S // tm),
        in_specs=[pl.BlockSpec(memory_space=pltpu.SMEM), xspec, _const_spec((1, D)), _const_spec((1, D)),
                  tspec, tspec, cur, prev, cur, prev, _const_spec(wq.shape), _const_spec(wo.shape)],
        out_specs=xspec,
        out_shape=jax.ShapeDtypeStruct(x.shape, x.dtype),
        scratch_shapes=[pltpu.VMEM((N_Q_HEADS, BLOCK, LANES), BF16), pltpu.VMEM((N_Q_HEADS, BLOCK, LANES), BF16),
                        pltpu.VMEM((tm + BLOCK, W), BF16), pltpu.VMEM((tm + BLOCK, 2 * W), BF16),
                        pltpu.VMEM((tm, D), BF16)],
        compiler_params=_params(),
        name="attn_layer",
    )(sinks, x, gpre, gpost, cos, sin, k, k, v, v, wq, wo)


def _pair_layout_q(wq):
    D = wq.shape[0]
    w = wq.reshape(D, N_Q_HEADS // 2, 2, 2, HALF)
    w = jnp.swapaxes(w, 2, 3)
    return w.reshape(D, N_Q_HEADS * HEAD_DIM)


def _dup_layout_k(wk):
    D = wk.shape[0]
    w = wk.reshape(D, N_KV_HEADS, 2, 1, HALF)
    w = jnp.broadcast_to(w, (D, N_KV_HEADS, 2, 2, HALF))
    return w.reshape(D, N_KV_HEADS * LANES)


def _dup_layout_v(wv):
    D = wv.shape[0]
    w = wv.reshape(D, N_KV_HEADS, 1, HEAD_DIM)
    w = jnp.broadcast_to(w, (D, N_KV_HEADS, 2, HEAD_DIM))
    return w.reshape(D, N_KV_HEADS * LANES)


def kernel(x, positions, mix_pre_g, mix_post_g, pool_w, pool_scale, kv_norm_g, w_kv, w_q, w_o, sinks,
           ffn_pre_g, ffn_post_g, ffn_w_in, ffn_conv_w, ffn_conv_b, ffn_w_out):
    B, S, D = x.shape
    tm = min(TILE_ROWS, S)
    tm_ffn = min(FFN_TILE_ROWS, S)
    tm_attn = min(ATTN_TILE_ROWS, S)
    depth = mix_pre_g.shape[0]
    n_a = pool_w.shape[0]
    F = ffn_w_out.shape[1]
    hkvd = N_KV_HEADS * HEAD_DIM
    assert S % tm == 0 and S % tm_ffn == 0 and S % tm_attn == 0 and tm_attn % (2 * BLOCK) == 0 and F % FFN_CHUNK == 0

    cos = sin = k = v = None
    for layer in range(depth):
        gpre, gpost = mix_pre_g[layer][None, :], mix_post_g[layer][None, :]
        if layer < n_a:
            x = _pool_layer(x, gpre, gpost, pool_w[layer].astype(BF16), pool_scale[layer][None, :], tm=tm)
        else:
            if layer == n_a:
                cos, sin = _rope_tables(positions, tm=tm)
                k, v = _kv_proj(x, kv_norm_g[None, :], cos, sin,
                                _dup_layout_k(w_kv[:, :hkvd]).astype(BF16),
                                _dup_layout_v(w_kv[:, hkvd:]).astype(BF16), tm=tm)
            j = layer - n_a
            x = _attn_layer(x, gpre, gpost, cos, sin, k, v, _pair_layout_q(w_q[j]).astype(BF16),
                            w_o[j].astype(BF16), sinks[j], tm=tm_attn)
        x = _ffn_layer(x, ffn_pre_g[layer][None, :], ffn_post_g[layer][None, :],
                       ffn_w_in[layer].astype(BF16), ffn_conv_w[layer], ffn_conv_b[layer][None, :],
                       ffn_w_out[layer].astype(BF16), tm=tm_ffn)
    return x
```

```python
import functools
import math

import jax
import jax.numpy as jnp
from jax import lax
from jax.experimental import pallas as pl
from jax.experimental.pallas import tpu as pltpu

F32 = jnp.float32
BF16 = jnp.bfloat16

RMS_EPS = 1e-6
POOL_WINDOWS = (2, 4, 8, 16)
HEAD_DIM = 64
HALF = HEAD_DIM // 2
N_Q_HEADS = 16
N_KV_HEADS = 4
Q_PER_KV = N_Q_HEADS // N_KV_HEADS
WINDOW = 128
BLOCK = 128
ROPE_THETA = 10000.0
ATTN_SCALE = 1.0 / math.sqrt(HEAD_DIM)
NEG_INF = -1e30
CONV_WIDTH = 3
LANES = 128
SUBLANES = 8
STAGE_PAD = 8
FFN_CHUNK = 256
FFN_SUBTILES = 2
VMEM_LIMIT = 56 * 1024 * 1024

TILE_ROWS = 1024
FFN_TILE_ROWS = 1024
ATTN_TILE_ROWS = 1024

GELU_C = math.sqrt(2.0 / math.pi)
LOG2E = math.log2(math.e)


def _rms(x, g):
    ms = jnp.mean(x * x, axis=-1, keepdims=True)
    return x * lax.rsqrt(ms + RMS_EPS) * g


def _const_spec(shape):
    nd = len(shape)
    return pl.BlockSpec(shape, lambda b, s: (0,) * nd, pipeline_mode=pl.Buffered(1))


def _layer_spec(shape, layer):
    nd = len(shape) - 1
    return pl.BlockSpec((None,) + tuple(shape[1:]), lambda b, s: (layer,) + (0,) * nd, pipeline_mode=pl.Buffered(1))


def _params():
    return pltpu.CompilerParams(dimension_semantics=("arbitrary", "arbitrary"),
                                vmem_limit_bytes=VMEM_LIMIT)


def _pool_kernel(x_ref, gpre_ref, gpost_ref, wp_ref, scale_ref, o_ref, stage, hp, hprev, *, tm, gc):
    s_idx = pl.program_id(1)
    D = hp.shape[1]
    Q = tm // SUBLANES
    P = Q + STAGE_PAD
    nslab = D // LANES
    wmax = max(POOL_WINDOWS)
    tail = slice(tm - wmax * SUBLANES, tm)

    @pl.when(s_idx == 0)
    def _():
        hprev[...] = jnp.zeros(hprev.shape, F32)

    x = x_ref[0]
    h = _rms(x, gpre_ref[...])
    for j in range(nslab):
        for i in range(SUBLANES):
            stage[j, i * P:i * P + Q, :] = h[i * Q:(i + 1) * Q, j * LANES:(j + 1) * LANES]
    for k in range(Q):
        hp[k * SUBLANES:(k + 1) * SUBLANES, :] = jnp.concatenate(
            [stage[j, pl.ds(k, SUBLANES, stride=P), :] for j in range(nslab)], axis=1)

    sub = lax.broadcasted_iota(jnp.int32, (wmax * SUBLANES, D), 0) % SUBLANES
    wrap = jnp.where(sub == SUBLANES - 1, hprev[...], hp[tail, :])
    wrap = jnp.concatenate([pltpu.roll(wrap[m * SUBLANES:(m + 1) * SUBLANES, :], 1, 0) for m in range(wmax)], axis=0)
    hprev[...] = hp[tail, :]

    first = (s_idx == 0).astype(F32)
    sub8 = lax.broadcasted_iota(jnp.int32, (SUBLANES, gc), 0)
    ys = []
    for gi, w in enumerate(POOL_WINDOWS):
        cols = slice(gi * gc, (gi + 1) * gc)
        own = hp[:, cols]
        acc = own
        for sft in range(1, w):
            acc = acc + jnp.concatenate([wrap[(wmax - sft) * SUBLANES:, cols], hp[0:tm - sft * SUBLANES, cols]], axis=0)
        mean = acc * (1.0 / w)
        fix = [jnp.where(sub8 == 0, 1.0 + first * (w / (k + 1.0) - 1.0), 1.0) for k in range(w - 1)]
        head = jnp.concatenate(fix, axis=0) * mean[0:(w - 1) * SUBLANES, :]
        mean = jnp.concatenate([head, mean[(w - 1) * SUBLANES:, :]], axis=0)
        d = mean - own
        ys.append(jnp.dot(d.astype(BF16), wp_ref[gi], preferred_element_type=F32))
    m = jnp.concatenate(ys, axis=-1) * scale_ref[...]
    r = _rms(m, gpost_ref[...])
    for k in range(Q):
        for j in range(nslab):
            stage[j, pl.ds(k, SUBLANES, stride=P), :] = r[k * SUBLANES:(k + 1) * SUBLANES, j * LANES:(j + 1) * LANES]
    rn = jnp.concatenate([jnp.concatenate([stage[j, i * P:i * P + Q, :] for i in range(SUBLANES)], axis=0)
                          for j in range(nslab)], axis=1)
    o_ref[0] = x + rn


def _pool_layer(x, gpre, gpost, wp, scale, *, tm):
    B, S, D = x.shape
    gc = D // len(POOL_WINDOWS)
    wmax = max(POOL_WINDOWS)
    assert tm // SUBLANES >= wmax
    xspec = pl.BlockSpec((1, tm, D), lambda b, s: (b, s, 0))
    return pl.pallas_call(
        functools.partial(_pool_kernel, tm=tm, gc=gc),
        grid=(B, S // tm),
        in_specs=[xspec, _const_spec((1, D)), _const_spec((1, D)),
                  _const_spec(wp.shape), _const_spec((1, D))],
        out_specs=xspec,
        out_shape=jax.ShapeDtypeStruct(x.shape, x.dtype),
        scratch_shapes=[pltpu.VMEM((D // LANES, tm + SUBLANES * STAGE_PAD, LANES), F32),
                        pltpu.VMEM((tm, D), F32), pltpu.VMEM((wmax * SUBLANES, D), F32)],
        compiler_params=_params(),
        name="pool_layer",
    )(x, gpre, gpost, wp, scale)


def _gelu_gate(gate, val):
    inner = gate * (GELU_C + (GELU_C * 0.044715) * (gate * gate))
    return (gate * val) * (0.5 * jnp.tanh(inner) + 0.5)


def _ffn_kernel(x_ref, gpre_ref, gpost_ref, win_ref, cw_ref, cb_ref, wout_ref, o_ref,
                stage_in, stage_out, h_scr, act_scr, halo, *, tm, fc, nc):
    F = nc * fc
    D = wout_ref.shape[1]
    sub_rows = tm // FFN_SUBTILES
    Q = sub_rows // SUBLANES
    P = Q + STAGE_PAD
    nslab = D // LANES

    @pl.when(pl.program_id(1) == 0)
    def _():
        halo[...] = jnp.zeros(halo.shape, F32)

    sub = lax.broadcasted_iota(jnp.int32, (SUBLANES, fc), 0)

    def rows(t):
        return slice(t * sub_rows, (t + 1) * sub_rows)

    def pre_norm(t):
        h = _rms(x_ref[0, rows(t), :], gpre_ref[...])
        for j in range(nslab):
            for i in range(SUBLANES):
                stage_in[t, j, i * P:i * P + Q, :] = h[i * Q:(i + 1) * Q, j * LANES:(j + 1) * LANES]
        for k in range(0, Q, 2):
            blk = [jnp.concatenate([stage_in[t, j, pl.ds(k + d, SUBLANES, stride=P), :] for j in range(nslab)], axis=1)
                   for d in range(2)]
            h_scr[t, k * SUBLANES:(k + 2) * SUBLANES, :] = jnp.concatenate(blk, axis=0).astype(BF16)

    def conv(t, col):
        u = jnp.dot(h_scr[t], win_ref[:, col:col + fc], preferred_element_type=F32)
        last1, last2 = u[sub_rows - SUBLANES:, :], u[sub_rows - 2 * SUBLANES:sub_rows - SUBLANES, :]
        b1 = pltpu.roll(jnp.where(sub == SUBLANES - 1, halo[0:SUBLANES, col:col + fc], last1), 1, 0)
        b2 = pltpu.roll(jnp.where(sub == SUBLANES - 1, halo[SUBLANES:2 * SUBLANES, col:col + fc], last2), 1, 0)
        halo[0:SUBLANES, col:col + fc] = last1
        halo[SUBLANES:2 * SUBLANES, col:col + fc] = last2
        um1 = jnp.concatenate([b1, u[:sub_rows - SUBLANES, :]], axis=0)
        um2 = jnp.concatenate([b2, b1, u[:sub_rows - 2 * SUBLANES, :]], axis=0)
        return (cw_ref[0:1, col:col + fc] * um2 + cw_ref[1:2, col:col + fc] * um1
                + cw_ref[2:3, col:col + fc] * u + cb_ref[:, col:col + fc])

    def up(t):
        for c in range(nc):
            gate = conv(t, c * fc)
            val = conv(t, F + c * fc)
            act_scr[t, :, c * fc:(c + 1) * fc] = _gelu_gate(gate, val).astype(BF16)

    def down(t):
        f = jnp.dot(act_scr[t], wout_ref[...], preferred_element_type=F32)
        for k in range(Q):
            for j in range(nslab):
                stage_out[t, j, pl.ds(k, SUBLANES, stride=P), :] = f[k * SUBLANES:(k + 1) * SUBLANES, j * LANES:(j + 1) * LANES]

    def post_norm(t):
        fn = jnp.concatenate([jnp.concatenate([stage_out[t, j, i * P:i * P + Q, :] for i in range(SUBLANES)], axis=0)
                              for j in range(nslab)], axis=1)
        o_ref[0, rows(t), :] = x_ref[0, rows(t), :] + _rms(fn, gpost_ref[...])

    pre_norm(0)
    for t in range(FFN_SUBTILES):
        up(t)
        if t + 1 < FFN_SUBTILES:
            pre_norm(t + 1)
        if t >= 1:
            post_norm(t - 1)
        down(t)
    post_norm(FFN_SUBTILES - 1)


def _ffn_layer(x, gpre, gpost, win, cw, cb, wout, *, layer, tm):
    B, S, D = x.shape
    F = wout.shape[1]
    fc = FFN_CHUNK
    nc = F // fc
    sub_rows = tm // FFN_SUBTILES
    assert cw.shape == (CONV_WIDTH, 2 * F) and sub_rows % (2 * SUBLANES) == 0
    xspec = pl.BlockSpec((1, tm, D), lambda b, s: (b, s, 0))
    stage = pltpu.VMEM((FFN_SUBTILES, D // LANES, sub_rows + SUBLANES * STAGE_PAD, LANES), F32)
    return pl.pallas_call(
        functools.partial(_ffn_kernel, tm=tm, fc=fc, nc=nc),
        grid=(B, S // tm),
        in_specs=[xspec, _const_spec((1, D)), _const_spec((1, D)), _layer_spec(win.shape, layer),
                  _const_spec(cw.shape), _const_spec(cb.shape), _layer_spec(wout.shape, layer)],
        out_specs=xspec,
        out_shape=jax.ShapeDtypeStruct(x.shape, x.dtype),
        scratch_shapes=[stage, stage,
                        pltpu.VMEM((FFN_SUBTILES, sub_rows, D), BF16), pltpu.VMEM((FFN_SUBTILES, sub_rows, F), BF16),
                        pltpu.VMEM(((CONV_WIDTH - 1) * SUBLANES, 2 * F), F32)],
        compiler_params=_params(),
        name="ffn_layer",
    )(x, gpre, gpost, win, cw, cb, wout)


def _rope_tables(pos, invf, sign, tm):
    ngrp = LANES // HALF
    q4 = tm // ngrp
    grp = lax.broadcasted_iota(jnp.int32, (q4, LANES), 1) // HALF
    packed = jnp.zeros((q4, LANES), F32)
    for g in range(ngrp):
        packed = jnp.where(grp == g, pos[g * q4:(g + 1) * q4, :], packed)
    ang = packed * invf
    tables = []
    for table in (jnp.cos(ang), jnp.sin(ang)):
        rolled = [table] + [pltpu.roll(table, HALF * d, 1) for d in range(1, ngrp)]
        quarters = []
        for g in range(ngrp):
            out = rolled[(0 - g) % ngrp]
            for h in range(1, ngrp):
                out = jnp.where(grp == h, rolled[(h - g) % ngrp], out)
            quarters.append(out)
        tables.append(jnp.concatenate(quarters, axis=0))
    return tables[0], tables[1] * sign


def _rope(t, cos, sin):
    return t * cos + pltpu.roll(t, LANES // 2, 1) * sin


def _kv_kernel(x_ref, g_ref, pos_ref, invf_ref, sign_ref, wk_ref, wv_ref, k_ref, v_ref, cos_ref, sin_ref, *, tm):
    cos, sin = _rope_tables(pos_ref[0].astype(F32), invf_ref[...], sign_ref[...], tm)
    cos_ref[0] = cos
    sin_ref[0] = sin
    h = _rms(x_ref[0], g_ref[...]).astype(BF16)
    kd = jnp.dot(h, wk_ref[...], preferred_element_type=F32)
    for j in range(N_KV_HEADS):
        k_ref[0, :, j * LANES:(j + 1) * LANES] = _rope(kd[:, j * LANES:(j + 1) * LANES], cos, sin).astype(BF16)
    v_ref[0] = jnp.dot(h, wv_ref[...], preferred_element_type=F32).astype(BF16)


def _kv_proj(x, g, positions, wk, wv, *, tm):
    B, S, D = x.shape
    W = N_KV_HEADS * LANES
    inv_freq = 1.0 / (ROPE_THETA ** (jnp.arange(0, HEAD_DIM, 2, dtype=F32) / HEAD_DIM))
    invf = jnp.tile(inv_freq, LANES // HALF)[None, :]
    sign = jnp.where(jnp.arange(LANES) < LANES // 2, -1.0, 1.0).astype(F32)[None, :]
    tspec = pl.BlockSpec((1, tm, LANES), lambda b, s: (b, s, 0))
    ospec = pl.BlockSpec((1, tm, W), lambda b, s: (b, s, 0))
    return pl.pallas_call(
        functools.partial(_kv_kernel, tm=tm),
        grid=(B, S // tm),
        in_specs=[pl.BlockSpec((1, tm, D), lambda b, s: (b, s, 0)), _const_spec((1, D)),
                  pl.BlockSpec((1, tm, 1), lambda b, s: (b, s, 0)), _const_spec((1, LANES)), _const_spec((1, LANES)),
                  _const_spec(wk.shape), _const_spec(wv.shape)],
        out_specs=[ospec, ospec, tspec, tspec],
        out_shape=[jax.ShapeDtypeStruct((B, S, W), BF16)] * 2 + [jax.ShapeDtypeStruct((B, S, LANES), F32)] * 2,
        compiler_params=_params(),
        name="kv_proj",
    )(x, g, positions[:, :, None], invf, sign, wk, wv)


def _attn_kernel(sinks_ref, x_ref, gpre_ref, gpost_ref, cos_ref, sin_ref, kc_ref, kp_ref, vc_ref, vp_ref,
                 wq_ref, wo_ref, o_ref, qa, qb, kbuf, vbuf, o_scr, *, tm):
    s_idx = pl.program_id(1)
    nblk = tm // BLOCK
    D = wq_ref.shape[0]
    cw = D // N_KV_HEADS
    lane = lax.broadcasted_iota(jnp.int32, (1, LANES), 1)
    first_of_pair = (lane % HEAD_DIM) < HALF

    def pre_norm(r0):
        return _rms(x_ref[0, pl.ds(r0, BLOCK), :], gpre_ref[...]).astype(BF16)

    def q_piece(h, j, r0, qdst):
        q = jnp.dot(h, wq_ref[:, j * cw:(j + 1) * cw], preferred_element_type=F32)
        cos, sin = cos_ref[0, pl.ds(r0, BLOCK), :], sin_ref[0, pl.ds(r0, BLOCK), :]
        for p in range(cw // LANES):
            qr = _rope(q[:, p * LANES:(p + 1) * LANES], cos, sin) * (ATTN_SCALE * LOG2E)
            qdst[Q_PER_KV * j + 2 * p] = jnp.where(first_of_pair, qr, 0.0).astype(BF16)
            qdst[Q_PER_KV * j + 2 * p + 1] = jnp.where(first_of_pair, 0.0, qr).astype(BF16)

    def post_norm(mo_parts, r0):
        mo = jnp.concatenate(mo_parts, axis=-1)
        o_ref[0, pl.ds(r0, BLOCK), :] = x_ref[0, pl.ds(r0, BLOCK), :] + _rms(mo, gpost_ref[...])

    kbuf[0:BLOCK, :] = kp_ref[0]
    kbuf[BLOCK:BLOCK + tm, :] = kc_ref[0]
    for j in range(N_KV_HEADS):
        vbuf[0:BLOCK, 2 * j * LANES:(2 * j + 1) * LANES] = vp_ref[0, :, j * LANES:(j + 1) * LANES]
        vbuf[BLOCK:BLOCK + tm, 2 * j * LANES:(2 * j + 1) * LANES] = vc_ref[0, :, j * LANES:(j + 1) * LANES]
        vbuf[:, (2 * j + 1) * LANES:(2 * j + 2) * LANES] = jnp.ones((tm + BLOCK, LANES), BF16)

    qi = lax.broadcasted_iota(jnp.int32, (BLOCK, 2 * BLOCK), 0)
    kj = lax.broadcasted_iota(jnp.int32, (BLOCK, 2 * BLOCK), 1)
    rel = BLOCK + qi - kj
    band = (rel >= 0) & (rel < WINDOW)
    out_lo = lax.broadcasted_iota(jnp.int32, (BLOCK, LANES), 1) < HEAD_DIM

    def block_step(jb, qcur, qnext, has_prev=True, has_next=True):
        r0 = pl.multiple_of(jb * BLOCK, BLOCK)
        rp = pl.multiple_of((jb - 1) * BLOCK, BLOCK) if has_prev else None
        rn = pl.multiple_of((jb + 1) * BLOCK, BLOCK) if has_next else None
        o_prev = o_scr[pl.ds(rp, BLOCK), :] if has_prev else None
        h_next = pre_norm(rn) if has_next else None
        seq_start = (s_idx * tm + r0 == 0).astype(jnp.int32)
        valid = band & (kj >= BLOCK * seq_start)
        mo_parts = []

        def scores(j):
            qblk = jnp.concatenate([qcur[Q_PER_KV * j + i] for i in range(Q_PER_KV)], axis=0)
            kblk = kbuf[pl.ds(r0, 2 * BLOCK), j * LANES:(j + 1) * LANES]
            return lax.dot_general(qblk, kblk, (((1,), (1,)), ((), ())), preferred_element_type=F32)

        def projections(j):
            if has_prev:
                mo_parts.append(jnp.dot(o_prev, wo_ref[:, j * cw:(j + 1) * cw], preferred_element_type=F32))
            if has_next:
                q_piece(h_next, j, rn, qnext)

        def values(j, sc):
            ps, sink_p = [], []
            for i in range(Q_PER_KV):
                sink = sinks_ref[Q_PER_KV * j + i] * LOG2E
                sh = jnp.where(valid, sc[i * BLOCK:(i + 1) * BLOCK], NEG_INF)
                m = jnp.maximum(jnp.max(sh, axis=-1, keepdims=True), sink)
                ps.append(jnp.exp2(sh - m).astype(BF16))
                sink_p.append(jnp.exp2(sink - m))
            vblk = vbuf[pl.ds(r0, 2 * BLOCK), 2 * j * LANES:(2 * j + 2) * LANES]
            ov = jnp.dot(jnp.concatenate(ps, axis=0), vblk, preferred_element_type=F32)
            heads = []
            for i in range(Q_PER_KV):
                num = ov[i * BLOCK:(i + 1) * BLOCK, :LANES]
                den = ov[i * BLOCK:(i + 1) * BLOCK, LANES:] + sink_p[i]
                heads.append(num / den)
            for pp in range(Q_PER_KV // 2):
                col = (Q_PER_KV // 2 * j + pp) * LANES
                o_scr[pl.ds(r0, BLOCK), col:col + LANES] = jnp.where(out_lo, heads[2 * pp], heads[2 * pp + 1]).astype(BF16)

        sc = scores(0)
        for j in range(N_KV_HEADS):
            sc_next = scores(j + 1) if j + 1 < N_KV_HEADS else None
            projections(j)
            values(j, sc)
            sc = sc_next
        if has_prev:
            post_norm(mo_parts, rp)

    h0 = pre_norm(0)
    for j in range(N_KV_HEADS):
        q_piece(h0, j, 0, qa)

    block_step(0, qa, qb, has_prev=False)

    def pair_body(i, carry):
        block_step(2 * i + 1, qb, qa)
        block_step(2 * i + 2, qa, qb)
        return carry

    lax.fori_loop(0, nblk // 2 - 1, pair_body, 0)
    block_step(nblk - 1, qb, qa, has_next=False)
    r_last = (nblk - 1) * BLOCK
    mo_last = jnp.dot(o_scr[r_last:r_last + BLOCK, :], wo_ref[...], preferred_element_type=F32)
    o_ref[0, r_last:r_last + BLOCK, :] = x_ref[0, r_last:r_last + BLOCK, :] + _rms(mo_last, gpost_ref[...])


def _attn_layer(x, gpre, gpost, cos, sin, k, v, wq, wo, sinks, *, layer, tm):
    B, S, D = x.shape
    W = N_KV_HEADS * LANES
    nbt = tm // BLOCK
    xspec = pl.BlockSpec((1, tm, D), lambda b, s: (b, s, 0))
    tspec = pl.BlockSpec((1, tm, LANES), lambda b, s: (b, s, 0))
    cur = pl.BlockSpec((1, tm, W), lambda b, s: (b, s, 0))
    prev = pl.BlockSpec((1, BLOCK, W), lambda b, s: (b, jnp.maximum(s * nbt - 1, 0), 0))
    return pl.pallas_call(
        functools.partial(_attn_kernel, tm=tm),
        grid=(B, S // tm),
        in_specs=[pl.BlockSpec(memory_space=pltpu.SMEM), xspec, _const_spec((1, D)), _const_spec((1, D)),
                  tspec, tspec, cur, prev, cur, prev, _layer_spec(wq.shape, layer), _layer_spec(wo.shape, layer)],
        out_specs=xspec,
        out_shape=jax.ShapeDtypeStruct(x.shape, x.dtype),
        scratch_shapes=[pltpu.VMEM((N_Q_HEADS, BLOCK, LANES), BF16), pltpu.VMEM((N_Q_HEADS, BLOCK, LANES), BF16),
                        pltpu.VMEM((tm + BLOCK, W), BF16), pltpu.VMEM((tm + BLOCK, 2 * W), BF16),
                        pltpu.VMEM((tm, D), BF16)],
        compiler_params=_params(),
        name="attn_layer",
    )(sinks, x, gpre, gpost, cos, sin, k, k, v, v, wq, wo)


def _pair_layout_q(wq):
    lead = wq.shape[:-1]
    w = wq.reshape(*lead, N_Q_HEADS // 2, 2, 2, HALF)
    w = jnp.swapaxes(w, -3, -2)
    return w.reshape(*lead, N_Q_HEADS * HEAD_DIM)


def _dup_layout_k(wk):
    D = wk.shape[0]
    w = wk.reshape(D, N_KV_HEADS, 2, 1, HALF)
    w = jnp.broadcast_to(w, (D, N_KV_HEADS, 2, 2, HALF))
    return w.reshape(D, N_KV_HEADS * LANES)


def _dup_layout_v(wv):
    D = wv.shape[0]
    w = wv.reshape(D, N_KV_HEADS, 1, HEAD_DIM)
    w = jnp.broadcast_to(w, (D, N_KV_HEADS, 2, HEAD_DIM))
    return w.reshape(D, N_KV_HEADS * LANES)


def kernel(x, positions, mix_pre_g, mix_post_g, pool_w, pool_scale, kv_norm_g, w_kv, w_q, w_o, sinks,
           ffn_pre_g, ffn_post_g, ffn_w_in, ffn_conv_w, ffn_conv_b, ffn_w_out):
    B, S, D = x.shape
    tm = min(TILE_ROWS, S)
    tm_ffn = min(FFN_TILE_ROWS, S)
    tm_attn = min(ATTN_TILE_ROWS, S)
    depth = mix_pre_g.shape[0]
    n_a = pool_w.shape[0]
    F = ffn_w_out.shape[1]
    hkvd = N_KV_HEADS * HEAD_DIM
    assert S % tm == 0 and S % tm_ffn == 0 and S % tm_attn == 0 and tm_attn % (2 * BLOCK) == 0 and F % FFN_CHUNK == 0

    win_all, wout_all = ffn_w_in.astype(BF16), ffn_w_out.astype(BF16)
    wq_all, wo_all = _pair_layout_q(w_q).astype(BF16), w_o.astype(BF16)
    cos = sin = k = v = None
    for layer in range(depth):
        gpre, gpost = mix_pre_g[layer][None, :], mix_post_g[layer][None, :]
        if layer < n_a:
            x = _pool_layer(x, gpre, gpost, pool_w[layer].astype(BF16), pool_scale[layer][None, :], tm=tm)
        else:
            if layer == n_a:
                k, v, cos, sin = _kv_proj(x, kv_norm_g[None, :], positions,
                                          _dup_layout_k(w_kv[:, :hkvd]).astype(BF16),
                                          _dup_layout_v(w_kv[:, hkvd:]).astype(BF16), tm=tm)
            j = layer - n_a
            x = _attn_layer(x, gpre, gpost, cos, sin, k, v, wq_all, wo_all, sinks[j], layer=j, tm=tm_attn)
        x = _ffn_layer(x, ffn_pre_g[layer][None, :], ffn_post_g[layer][None, :],
                       win_all, ffn_conv_w[layer], ffn_conv_b[layer][None, :], wout_all, layer=layer, tm=tm_ffn)
    return x
```

```python
import functools
import math

import jax
import jax.numpy as jnp
from jax import lax
from jax.experimental import pallas as pl
from jax.experimental.pallas import tpu as pltpu

F32 = jnp.float32
BF16 = jnp.bfloat16

RMS_EPS = 1e-6
POOL_WINDOWS = (2, 4, 8, 16)
HEAD_DIM = 64
HALF = HEAD_DIM // 2
N_Q_HEADS = 16
N_KV_HEADS = 4
Q_PER_KV = N_Q_HEADS // N_KV_HEADS
WINDOW = 128
BLOCK = 128
ROPE_THETA = 10000.0
ATTN_SCALE = 1.0 / math.sqrt(HEAD_DIM)
NEG_INF = -1e30
CONV_WIDTH = 3
LANES = 128
SUBLANES = 8
STAGE_PAD = 8
FFN_CHUNK = 256
FFN_SUBTILES = 2
VMEM_LIMIT = 56 * 1024 * 1024

TILE_ROWS = 1024
FFN_TILE_ROWS = 1024
ATTN_TILE_ROWS = 1024

GELU_C = math.sqrt(2.0 / math.pi)
LOG2E = math.log2(math.e)


def _rms(x, g):
    ms = jnp.mean(x * x, axis=-1, keepdims=True)
    return x * lax.rsqrt(ms + RMS_EPS) * g


def _const_spec(shape):
    nd = len(shape)
    return pl.BlockSpec(shape, lambda b, s: (0,) * nd, pipeline_mode=pl.Buffered(1))


def _layer_spec(shape, layer):
    nd = len(shape) - 1
    return pl.BlockSpec((None,) + tuple(shape[1:]), lambda b, s: (layer,) + (0,) * nd, pipeline_mode=pl.Buffered(1))


def _params():
    return pltpu.CompilerParams(dimension_semantics=("arbitrary", "arbitrary"),
                                vmem_limit_bytes=VMEM_LIMIT)


def _pool_kernel(x_ref, gpre_ref, gpost_ref, wp_ref, scale_ref, o_ref, stage, hp, *hprev, tm, gc):
    s_idx = pl.program_id(1)
    D = hp.shape[1]
    Q = tm // SUBLANES
    P = Q + STAGE_PAD
    nslab = D // LANES

    @pl.when(s_idx == 0)
    def _():
        for prev in hprev:
            prev[...] = jnp.zeros(prev.shape, F32)

    x = x_ref[0]
    h = _rms(x, gpre_ref[...])
    for j in range(nslab):
        for i in range(SUBLANES):
            stage[j, i * P:i * P + Q, :] = h[i * Q:(i + 1) * Q, j * LANES:(j + 1) * LANES]
    for k in range(Q):
        hp[k * SUBLANES:(k + 1) * SUBLANES, :] = jnp.concatenate(
            [stage[j, pl.ds(k, SUBLANES, stride=P), :] for j in range(nslab)], axis=1)

    def shifted(t, sft, prev):
        rows, cols = sft * SUBLANES, t.shape[1]
        last = t[tm - rows:, :]
        sub = lax.broadcasted_iota(jnp.int32, (rows, cols), 0) % SUBLANES
        wrap = jnp.where(sub == SUBLANES - 1, prev[...], last)
        wrap = jnp.concatenate([pltpu.roll(wrap[m * SUBLANES:(m + 1) * SUBLANES, :], 1, 0) for m in range(sft)], axis=0)
        prev[...] = last
        return jnp.concatenate([wrap, t[:tm - rows, :]], axis=0)

    level, sums = hp[...], []
    for gi, (w, prev) in enumerate(zip(POOL_WINDOWS, hprev)):
        assert w == 2 ** (gi + 1)
        level = level + shifted(level, w // 2, prev)
        sums.append(level[:, :gc])
        level = level[:, gc:]

    first = (s_idx == 0).astype(F32)
    sub8 = lax.broadcasted_iota(jnp.int32, (SUBLANES, gc), 0)
    ys = []
    for gi, w in enumerate(POOL_WINDOWS):
        own = hp[:, gi * gc:(gi + 1) * gc]
        mean = sums[gi] * (1.0 / w)
        fix = [jnp.where(sub8 == 0, 1.0 + first * (w / (k + 1.0) - 1.0), 1.0) for k in range(w - 1)]
        head = jnp.concatenate(fix, axis=0) * mean[0:(w - 1) * SUBLANES, :]
        mean = jnp.concatenate([head, mean[(w - 1) * SUBLANES:, :]], axis=0)
        d = mean - own
        ys.append(jnp.dot(d.astype(BF16), wp_ref[gi], preferred_element_type=F32))
    m = jnp.concatenate(ys, axis=-1) * scale_ref[...]
    r = _rms(m, gpost_ref[...])
    for k in range(Q):
        for j in range(nslab):
            stage[j, pl.ds(k, SUBLANES, stride=P), :] = r[k * SUBLANES:(k + 1) * SUBLANES, j * LANES:(j + 1) * LANES]
    rn = jnp.concatenate([jnp.concatenate([stage[j, i * P:i * P + Q, :] for i in range(SUBLANES)], axis=0)
                          for j in range(nslab)], axis=1)
    o_ref[0] = x + rn


def _pool_layer(x, gpre, gpost, wp, scale, *, tm):
    B, S, D = x.shape
    gc = D // len(POOL_WINDOWS)
    wmax = max(POOL_WINDOWS)
    assert tm // SUBLANES >= wmax
    xspec = pl.BlockSpec((1, tm, D), lambda b, s: (b, s, 0))
    return pl.pallas_call(
        functools.partial(_pool_kernel, tm=tm, gc=gc),
        grid=(B, S // tm),
        in_specs=[xspec, _const_spec((1, D)), _const_spec((1, D)),
                  _const_spec(wp.shape), _const_spec((1, D))],
        out_specs=xspec,
        out_shape=jax.ShapeDtypeStruct(x.shape, x.dtype),
        scratch_shapes=[pltpu.VMEM((D // LANES, tm + SUBLANES * STAGE_PAD, LANES), F32),
                        pltpu.VMEM((tm, D), F32)]
                       + [pltpu.VMEM((w // 2 * SUBLANES, D - gi * gc), F32) for gi, w in enumerate(POOL_WINDOWS)],
        compiler_params=_params(),
        name="pool_layer",
    )(x, gpre, gpost, wp, scale)


def _gelu_gate(gate, val):
    inner = gate * (GELU_C + (GELU_C * 0.044715) * (gate * gate))
    return (gate * val) * (0.5 * jnp.tanh(inner) + 0.5)


def _ffn_kernel(x_ref, gpre_ref, gpost_ref, win_ref, cw_ref, cb_ref, wout_ref, o_ref,
                stage_in, stage_out, h_scr, act_scr, halo, *, tm, fc, nc):
    F = nc * fc
    D = wout_ref.shape[1]
    sub_rows = tm // FFN_SUBTILES
    Q = sub_rows // SUBLANES
    P = Q + STAGE_PAD
    nslab = D // LANES

    @pl.when(pl.program_id(1) == 0)
    def _():
        halo[...] = jnp.zeros(halo.shape, F32)

    sub = lax.broadcasted_iota(jnp.int32, (SUBLANES, fc), 0)

    def rows(t):
        return slice(t * sub_rows, (t + 1) * sub_rows)

    def pre_norm(t):
        h = _rms(x_ref[0, rows(t), :], gpre_ref[...])
        for j in range(nslab):
            for i in range(SUBLANES):
                stage_in[t, j, i * P:i * P + Q, :] = h[i * Q:(i + 1) * Q, j * LANES:(j + 1) * LANES]
        for k in range(0, Q, 2):
            blk = [jnp.concatenate([stage_in[t, j, pl.ds(k + d, SUBLANES, stride=P), :] for j in range(nslab)], axis=1)
                   for d in range(2)]
            h_scr[t, k * SUBLANES:(k + 2) * SUBLANES, :] = jnp.concatenate(blk, axis=0).astype(BF16)

    def conv(t, col):
        u = jnp.dot(h_scr[t], win_ref[:, col:col + fc], preferred_element_type=F32)
        last1, last2 = u[sub_rows - SUBLANES:, :], u[sub_rows - 2 * SUBLANES:sub_rows - SUBLANES, :]
        b1 = pltpu.roll(jnp.where(sub == SUBLANES - 1, halo[0:SUBLANES, col:col + fc], last1), 1, 0)
        b2 = pltpu.roll(jnp.where(sub == SUBLANES - 1, halo[SUBLANES:2 * SUBLANES, col:col + fc], last2), 1, 0)
        halo[0:SUBLANES, col:col + fc] = last1
        halo[SUBLANES:2 * SUBLANES, col:col + fc] = last2
        um1 = jnp.concatenate([b1, u[:sub_rows - SUBLANES, :]], axis=0)
        um2 = jnp.concatenate([b2, b1, u[:sub_rows - 2 * SUBLANES, :]], axis=0)
        return (cw_ref[0:1, col:col + fc] * um2 + cw_ref[1:2, col:col + fc] * um1
                + cw_ref[2:3, col:col + fc] * u + cb_ref[:, col:col + fc])

    def up(t):
        for c in range(nc):
            gate = conv(t, c * fc)
            val = conv(t, F + c * fc)
            act_scr[t, :, c * fc:(c + 1) * fc] = _gelu_gate(gate, val).astype(BF16)

    def down(t):
        f = jnp.dot(act_scr[t], wout_ref[...], preferred_element_type=F32)
        for k in range(Q):
            for j in range(nslab):
                stage_out[t, j, pl.ds(k, SUBLANES, stride=P), :] = f[k * SUBLANES:(k + 1) * SUBLANES, j * LANES:(j + 1) * LANES]

    def post_norm(t):
        fn = jnp.concatenate([jnp.concatenate([stage_out[t, j, i * P:i * P + Q, :] for i in range(SUBLANES)], axis=0)
                              for j in range(nslab)], axis=1)
        o_ref[0, rows(t), :] = x_ref[0, rows(t), :] + _rms(fn, gpost_ref[...])

    pre_norm(0)
    for t in range(FFN_SUBTILES):
        up(t)
        if t + 1 < FFN_SUBTILES:
            pre_norm(t + 1)
        if t >= 1:
            post_norm(t - 1)
        down(t)
    post_norm(FFN_SUBTILES - 1)


def _ffn_layer(x, gpre, gpost, win, cw, cb, wout, *, layer, tm):
    B, S, D = x.shape
    F = wout.shape[1]
    fc = FFN_CHUNK
    nc = F // fc
    sub_rows = tm // FFN_SUBTILES
    assert cw.shape == (CONV_WIDTH, 2 * F) and sub_rows % (2 * SUBLANES) == 0
    xspec = pl.BlockSpec((1, tm, D), lambda b, s: (b, s, 0))
    stage = pltpu.VMEM((FFN_SUBTILES, D // LANES, sub_rows + SUBLANES * STAGE_PAD, LANES), F32)
    return pl.pallas_call(
        functools.partial(_ffn_kernel, tm=tm, fc=fc, nc=nc),
        grid=(B, S // tm),
        in_specs=[xspec, _const_spec((1, D)), _const_spec((1, D)), _layer_spec(win.shape, layer),
                  _const_spec(cw.shape), _const_spec(cb.shape), _layer_spec(wout.shape, layer)],
        out_specs=xspec,
        out_shape=jax.ShapeDtypeStruct(x.shape, x.dtype),
        scratch_shapes=[stage, stage,
                        pltpu.VMEM((FFN_SUBTILES, sub_rows, D), BF16), pltpu.VMEM((FFN_SUBTILES, sub_rows, F), BF16),
                        pltpu.VMEM(((CONV_WIDTH - 1) * SUBLANES, 2 * F), F32)],
        compiler_params=_params(),
        name="ffn_layer",
    )(x, gpre, gpost, win, cw, cb, wout)


def _rope_tables(pos, invf, sign, tm):
    ngrp = LANES // HALF
    q4 = tm // ngrp
    grp = lax.broadcasted_iota(jnp.int32, (q4, LANES), 1) // HALF
    packed = jnp.zeros((q4, LANES), F32)
    for g in range(ngrp):
        packed = jnp.where(grp == g, pos[g * q4:(g + 1) * q4, :], packed)
    ang = packed * invf
    tables = []
    for table in (jnp.cos(ang), jnp.sin(ang)):
        rolled = [table] + [pltpu.roll(table, HALF * d, 1) for d in range(1, ngrp)]
        quarters = []
        for g in range(ngrp):
            out = rolled[(0 - g) % ngrp]
            for h in range(1, ngrp):
                out = jnp.where(grp == h, rolled[(h - g) % ngrp], out)
            quarters.append(out)
        tables.append(jnp.concatenate(quarters, axis=0))
    return tables[0], tables[1] * sign


def _rope(t, cos, sin):
    return t * cos + pltpu.roll(t, LANES // 2, 1) * sin


def _kv_kernel(x_ref, g_ref, pos_ref, invf_ref, sign_ref, wk_ref, wv_ref, k_ref, v_ref, cos_ref, sin_ref, *, tm):
    cos, sin = _rope_tables(pos_ref[0].astype(F32), invf_ref[...], sign_ref[...], tm)
    cos_ref[0] = cos
    sin_ref[0] = sin
    h = _rms(x_ref[0], g_ref[...]).astype(BF16)
    kd = jnp.dot(h, wk_ref[...], preferred_element_type=F32)
    for j in range(N_KV_HEADS):
        k_ref[0, :, j * LANES:(j + 1) * LANES] = _rope(kd[:, j * LANES:(j + 1) * LANES], cos, sin).astype(BF16)
    v_ref[0] = jnp.dot(h, wv_ref[...], preferred_element_type=F32).astype(BF16)


def _kv_proj(x, g, positions, wk, wv, *, tm):
    B, S, D = x.shape
    W = N_KV_HEADS * LANES
    inv_freq = 1.0 / (ROPE_THETA ** (jnp.arange(0, HEAD_DIM, 2, dtype=F32) / HEAD_DIM))
    invf = jnp.tile(inv_freq, LANES // HALF)[None, :]
    sign = jnp.where(jnp.arange(LANES) < LANES // 2, -1.0, 1.0).astype(F32)[None, :]
    tspec = pl.BlockSpec((1, tm, LANES), lambda b, s: (b, s, 0))
    ospec = pl.BlockSpec((1, tm, W), lambda b, s: (b, s, 0))
    return pl.pallas_call(
        functools.partial(_kv_kernel, tm=tm),
        grid=(B, S // tm),
        in_specs=[pl.BlockSpec((1, tm, D), lambda b, s: (b, s, 0)), _const_spec((1, D)),
                  pl.BlockSpec((1, tm, 1), lambda b, s: (b, s, 0)), _const_spec((1, LANES)), _const_spec((1, LANES)),
                  _const_spec(wk.shape), _const_spec(wv.shape)],
        out_specs=[ospec, ospec, tspec, tspec],
        out_shape=[jax.ShapeDtypeStruct((B, S, W), BF16)] * 2 + [jax.ShapeDtypeStruct((B, S, LANES), F32)] * 2,
        compiler_params=_params(),
        name="kv_proj",
    )(x, g, positions[:, :, None], invf, sign, wk, wv)


def _attn_kernel(sinks_ref, x_ref, gpre_ref, gpost_ref, cos_ref, sin_ref, kc_ref, kp_ref, vc_ref, vp_ref,
                 wq_ref, wo_ref, o_ref, qa, qb, kbuf, vbuf, o_scr, *, tm):
    s_idx = pl.program_id(1)
    nblk = tm // BLOCK
    D = wq_ref.shape[0]
    cw = D // N_KV_HEADS
    lane = lax.broadcasted_iota(jnp.int32, (1, LANES), 1)
    first_of_pair = (lane % HEAD_DIM) < HALF

    def pre_norm(r0):
        return _rms(x_ref[0, pl.ds(r0, BLOCK), :], gpre_ref[...]).astype(BF16)

    def q_piece(h, j, r0, qdst):
        q = jnp.dot(h, wq_ref[:, j * cw:(j + 1) * cw], preferred_element_type=F32)
        cos, sin = cos_ref[0, pl.ds(r0, BLOCK), :], sin_ref[0, pl.ds(r0, BLOCK), :]
        for p in range(cw // LANES):
            qr = _rope(q[:, p * LANES:(p + 1) * LANES], cos, sin) * (ATTN_SCALE * LOG2E)
            qdst[Q_PER_KV * j + 2 * p] = jnp.where(first_of_pair, qr, 0.0).astype(BF16)
            qdst[Q_PER_KV * j + 2 * p + 1] = jnp.where(first_of_pair, 0.0, qr).astype(BF16)

    def post_norm(mo_parts, r0):
        mo = jnp.concatenate(mo_parts, axis=-1)
        o_ref[0, pl.ds(r0, BLOCK), :] = x_ref[0, pl.ds(r0, BLOCK), :] + _rms(mo, gpost_ref[...])

    kbuf[0:BLOCK, :] = kp_ref[0]
    kbuf[BLOCK:BLOCK + tm, :] = kc_ref[0]
    for j in range(N_KV_HEADS):
        vbuf[0:BLOCK, 2 * j * LANES:(2 * j + 1) * LANES] = vp_ref[0, :, j * LANES:(j + 1) * LANES]
        vbuf[BLOCK:BLOCK + tm, 2 * j * LANES:(2 * j + 1) * LANES] = vc_ref[0, :, j * LANES:(j + 1) * LANES]
        vbuf[:, (2 * j + 1) * LANES:(2 * j + 2) * LANES] = jnp.ones((tm + BLOCK, LANES), BF16)

    qi = lax.broadcasted_iota(jnp.int32, (BLOCK, 2 * BLOCK), 0)
    kj = lax.broadcasted_iota(jnp.int32, (BLOCK, 2 * BLOCK), 1)
    rel = BLOCK + qi - kj
    band = (rel >= 0) & (rel < WINDOW)
    out_lo = lax.broadcasted_iota(jnp.int32, (BLOCK, LANES), 1) < HEAD_DIM

    def block_step(jb, qcur, qnext, has_prev=True, has_next=True):
        r0 = pl.multiple_of(jb * BLOCK, BLOCK)
        rp = pl.multiple_of((jb - 1) * BLOCK, BLOCK) if has_prev else None
        rn = pl.multiple_of((jb + 1) * BLOCK, BLOCK) if has_next else None
        o_prev = o_scr[pl.ds(rp, BLOCK), :] if has_prev else None
        h_next = pre_norm(rn) if has_next else None
        seq_start = (s_idx * tm + r0 == 0).astype(jnp.int32)
        valid = band & (kj >= BLOCK * seq_start)
        mo_parts = []

        def scores(j):
            qblk = jnp.concatenate([qcur[Q_PER_KV * j + i] for i in range(Q_PER_KV)], axis=0)
            kblk = kbuf[pl.ds(r0, 2 * BLOCK), j * LANES:(j + 1) * LANES]
            return lax.dot_general(qblk, kblk, (((1,), (1,)), ((), ())), preferred_element_type=F32)

        def projections(j):
            if has_prev:
                mo_parts.append(jnp.dot(o_prev, wo_ref[:, j * cw:(j + 1) * cw], preferred_element_type=F32))
            if has_next:
                q_piece(h_next, j, rn, qnext)

        def values(j, sc):
            ps, sink_p = [], []
            for i in range(Q_PER_KV):
                sink = sinks_ref[Q_PER_KV * j + i] * LOG2E
                sh = jnp.where(valid, sc[i * BLOCK:(i + 1) * BLOCK], NEG_INF)
                m = jnp.maximum(jnp.max(sh, axis=-1, keepdims=True), sink)
                ps.append(jnp.exp2(sh - m).astype(BF16))
                sink_p.append(jnp.exp2(sink - m))
            vblk = vbuf[pl.ds(r0, 2 * BLOCK), 2 * j * LANES:(2 * j + 2) * LANES]
            ov = jnp.dot(jnp.concatenate(ps, axis=0), vblk, preferred_element_type=F32)
            heads = []
            for i in range(Q_PER_KV):
                num = ov[i * BLOCK:(i + 1) * BLOCK, :LANES]
                den = ov[i * BLOCK:(i + 1) * BLOCK, LANES:] + sink_p[i]
                heads.append(num / den)
            for pp in range(Q_PER_KV // 2):
                col = (Q_PER_KV // 2 * j + pp) * LANES
                o_scr[pl.ds(r0, BLOCK), col:col + LANES] = jnp.where(out_lo, heads[2 * pp], heads[2 * pp + 1]).astype(BF16)

        sc = scores(0)
        for j in range(N_KV_HEADS):
            sc_next = scores(j + 1) if j + 1 < N_KV_HEADS else None
            projections(j)
            values(j, sc)
            sc = sc_next
        if has_prev:
            post_norm(mo_parts, rp)

    h0 = pre_norm(0)
    for j in range(N_KV_HEADS):
        q_piece(h0, j, 0, qa)

    block_step(0, qa, qb, has_prev=False)

    def pair_body(i, carry):
        block_step(2 * i + 1, qb, qa)
        block_step(2 * i + 2, qa, qb)
        return carry

    lax.fori_loop(0, nblk // 2 - 1, pair_body, 0)
    block_step(nblk - 1, qb, qa, has_next=False)
    r_last = (nblk - 1) * BLOCK
    mo_last = jnp.dot(o_scr[r_last:r_last + BLOCK, :], wo_ref[...], preferred_element_type=F32)
    o_ref[0, r_last:r_last + BLOCK, :] = x_ref[0, r_last:r_last + BLOCK, :] + _rms(mo_last, gpost_ref[...])


def _attn_layer(x, gpre, gpost, cos, sin, k, v, wq, wo, sinks, *, layer, tm):
    B, S, D = x.shape
    W = N_KV_HEADS * LANES
    nbt = tm // BLOCK
    xspec = pl.BlockSpec((1, tm, D), lambda b, s: (b, s, 0))
    tspec = pl.BlockSpec((1, tm, LANES), lambda b, s: (b, s, 0))
    cur = pl.BlockSpec((1, tm, W), lambda b, s: (b, s, 0))
    prev = pl.BlockSpec((1, BLOCK, W), lambda b, s: (b, jnp.maximum(s * nbt - 1, 0), 0))
    return pl.pallas_call(
        functools.partial(_attn_kernel, tm=tm),
        grid=(B, S // tm),
        in_specs=[pl.BlockSpec(memory_space=pltpu.SMEM), xspec, _const_spec((1, D)), _const_spec((1, D)),
                  tspec, tspec, cur, prev, cur, prev, _layer_spec(wq.shape, layer), _layer_spec(wo.shape, layer)],
        out_specs=xspec,
        out_shape=jax.ShapeDtypeStruct(x.shape, x.dtype),
        scratch_shapes=[pltpu.VMEM((N_Q_HEADS, BLOCK, LANES), BF16), pltpu.VMEM((N_Q_HEADS, BLOCK, LANES), BF16),
                        pltpu.VMEM((tm + BLOCK, W), BF16), pltpu.VMEM((tm + BLOCK, 2 * W), BF16),
                        pltpu.VMEM((tm, D), BF16)],
        compiler_params=_params(),
        name="attn_layer",
    )(sinks, x, gpre, gpost, cos, sin, k, k, v, v, wq, wo)


def _pair_layout_q(wq):
    lead = wq.shape[:-1]
    w = wq.reshape(*lead, N_Q_HEADS // 2, 2, 2, HALF)
    w = jnp.swapaxes(w, -3, -2)
    return w.reshape(*lead, N_Q_HEADS * HEAD_DIM)


def _dup_layout_k(wk):
    D = wk.shape[0]
    w = wk.reshape(D, N_KV_HEADS, 2, 1, HALF)
    w = jnp.broadcast_to(w, (D, N_KV_HEADS, 2, 2, HALF))
    return w.reshape(D, N_KV_HEADS * LANES)


def _dup_layout_v(wv):
    D = wv.shape[0]
    w = wv.reshape(D, N_KV_HEADS, 1, HEAD_DIM)
    w = jnp.broadcast_to(w, (D, N_KV_HEADS, 2, HEAD_DIM))
    return w.reshape(D, N_KV_HEADS * LANES)


def kernel(x, positions, mix_pre_g, mix_post_g, pool_w, pool_scale, kv_norm_g, w_kv, w_q, w_o, sinks,
           ffn_pre_g, ffn_post_g, ffn_w_in, ffn_conv_w, ffn_conv_b, ffn_w_out):
    B, S, D = x.shape
    tm = min(TILE_ROWS, S)
    tm_ffn = min(FFN_TILE_ROWS, S)
    tm_attn = min(ATTN_TILE_ROWS, S)
    depth = mix_pre_g.shape[0]
    n_a = pool_w.shape[0]
    F = ffn_w_out.shape[1]
    hkvd = N_KV_HEADS * HEAD_DIM
    assert S % tm == 0 and S % tm_ffn == 0 and S % tm_attn == 0 and tm_attn % (2 * BLOCK) == 0 and F % FFN_CHUNK == 0

    win_all, wout_all = ffn_w_in.astype(BF16), ffn_w_out.astype(BF16)
    wq_all, wo_all = _pair_layout_q(w_q).astype(BF16), w_o.astype(BF16)
    cos = sin = k = v = None
    for layer in range(depth):
        gpre, gpost = mix_pre_g[layer][None, :], mix_post_g[layer][None, :]
        if layer < n_a:
            x = _pool_layer(x, gpre, gpost, pool_w[layer].astype(BF16), pool_scale[layer][None, :], tm=tm)
        else:
            if layer == n_a:
                k, v, cos, sin = _kv_proj(x, kv_norm_g[None, :], positions,
                                          _dup_layout_k(w_kv[:, :hkvd]).astype(BF16),
                                          _dup_layout_v(w_kv[:, hkvd:]).astype(BF16), tm=tm)
            j = layer - n_a
            x = _attn_layer(x, gpre, gpost, cos, sin, k, v, wq_all, wo_all, sinks[j], layer=j, tm=tm_attn)
        x = _ffn_layer(x, ffn_pre_g[layer][None, :], ffn_post_g[layer][None, :],
                       win_all, ffn_conv_w[layer], ffn_conv_b[layer][None, :], wout_all, layer=layer, tm=tm_ffn)
    return x
```

```python
import functools
import math

import jax
import jax.numpy as jnp
from jax import lax
from jax.experimental import pallas as pl
from jax.experimental.pallas import tpu as pltpu

F32 = jnp.float32
BF16 = jnp.bfloat16

RMS_EPS = 1e-6
POOL_WINDOWS = (2, 4, 8, 16)
HEAD_DIM = 64
HALF = HEAD_DIM // 2
N_Q_HEADS = 16
N_KV_HEADS = 4
Q_PER_KV = N_Q_HEADS // N_KV_HEADS
WINDOW = 128
BLOCK = 128
ROPE_THETA = 10000.0
ATTN_SCALE = 1.0 / math.sqrt(HEAD_DIM)
NEG_INF = -1e30
CONV_WIDTH = 3
LANES = 128
SUBLANES = 8
STAGE_PAD = 8
FFN_CHUNK = 256
FFN_SUBTILES = 2
VMEM_LIMIT = 56 * 1024 * 1024

TILE_ROWS = 1024
FFN_TILE_ROWS = 1024
ATTN_TILE_ROWS = 1024

GELU_C = math.sqrt(2.0 / math.pi)
LOG2E = math.log2(math.e)


def _rms(x, g):
    ms = jnp.mean(x * x, axis=-1, keepdims=True)
    return x * lax.rsqrt(ms + RMS_EPS) * g


def _const_spec(shape):
    nd = len(shape)
    return pl.BlockSpec(shape, lambda b, s: (0,) * nd, pipeline_mode=pl.Buffered(1))


def _layer_spec(shape, layer):
    nd = len(shape) - 1
    return pl.BlockSpec((None,) + tuple(shape[1:]), lambda b, s: (layer,) + (0,) * nd, pipeline_mode=pl.Buffered(1))


def _params():
    return pltpu.CompilerParams(dimension_semantics=("arbitrary", "arbitrary"),
                                vmem_limit_bytes=VMEM_LIMIT)


def _pool_kernel(x_ref, gpre_ref, gpost_ref, wp_ref, scale_ref, win_ref, wout_ref, o_ref, win_bf_ref, wout_bf_ref,
                 stage, hp, *hprev, tm, gc):
    win_bf_ref[...] = win_ref[...].astype(BF16)
    wout_bf_ref[...] = wout_ref[...].astype(BF16)
    s_idx = pl.program_id(1)
    D = hp.shape[1]
    Q = tm // SUBLANES
    P = Q + STAGE_PAD
    nslab = D // LANES

    @pl.when(s_idx == 0)
    def _():
        for prev in hprev:
            prev[...] = jnp.zeros(prev.shape, F32)

    x = x_ref[0]
    h = _rms(x, gpre_ref[...])
    for j in range(nslab):
        for i in range(SUBLANES):
            stage[j, i * P:i * P + Q, :] = h[i * Q:(i + 1) * Q, j * LANES:(j + 1) * LANES]
    for k in range(Q):
        hp[k * SUBLANES:(k + 1) * SUBLANES, :] = jnp.concatenate(
            [stage[j, pl.ds(k, SUBLANES, stride=P), :] for j in range(nslab)], axis=1)

    def shifted(t, sft, prev):
        rows, cols = sft * SUBLANES, t.shape[1]
        last = t[tm - rows:, :]
        sub = lax.broadcasted_iota(jnp.int32, (rows, cols), 0) % SUBLANES
        wrap = jnp.where(sub == SUBLANES - 1, prev[...], last)
        wrap = jnp.concatenate([pltpu.roll(wrap[m * SUBLANES:(m + 1) * SUBLANES, :], 1, 0) for m in range(sft)], axis=0)
        prev[...] = last
        return jnp.concatenate([wrap, t[:tm - rows, :]], axis=0)

    level, sums = hp[...], []
    for gi, (w, prev) in enumerate(zip(POOL_WINDOWS, hprev)):
        assert w == 2 ** (gi + 1)
        level = level + shifted(level, w // 2, prev)
        sums.append(level[:, :gc])
        level = level[:, gc:]

    first = (s_idx == 0).astype(F32)
    sub8 = lax.broadcasted_iota(jnp.int32, (SUBLANES, gc), 0)
    ys = []
    for gi, w in enumerate(POOL_WINDOWS):
        own = hp[:, gi * gc:(gi + 1) * gc]
        mean = sums[gi] * (1.0 / w)
        fix = [jnp.where(sub8 == 0, 1.0 + first * (w / (k + 1.0) - 1.0), 1.0) for k in range(w - 1)]
        head = jnp.concatenate(fix, axis=0) * mean[0:(w - 1) * SUBLANES, :]
        mean = jnp.concatenate([head, mean[(w - 1) * SUBLANES:, :]], axis=0)
        d = mean - own
        ys.append(jnp.dot(d.astype(BF16), wp_ref[gi], preferred_element_type=F32))
    m = jnp.concatenate(ys, axis=-1) * scale_ref[...]
    r = _rms(m, gpost_ref[...])
    for k in range(Q):
        for j in range(nslab):
            stage[j, pl.ds(k, SUBLANES, stride=P), :] = r[k * SUBLANES:(k + 1) * SUBLANES, j * LANES:(j + 1) * LANES]
    rn = jnp.concatenate([jnp.concatenate([stage[j, i * P:i * P + Q, :] for i in range(SUBLANES)], axis=0)
                          for j in range(nslab)], axis=1)
    o_ref[0] = x + rn


def _pool_layer(x, gpre, gpost, wp, scale, win, wout, part, nparts, *, tm):
    B, S, D = x.shape
    gc = D // len(POOL_WINDOWS)
    wmax = max(POOL_WINDOWS)
    assert tm // SUBLANES >= wmax
    ns = S // tm
    nsteps = B * ns
    xspec = pl.BlockSpec((1, tm, D), lambda b, s: (b, s, 0))

    def cast_specs(w):
        rows = w.shape[0] // nparts
        blk = rows // nsteps
        assert w.shape[0] % nparts == 0 and rows % nsteps == 0 and blk % (2 * SUBLANES) == 0
        return (pl.BlockSpec((blk, w.shape[1]), lambda b, s: (part * nsteps + b * ns + s, 0)),
                pl.BlockSpec((blk, w.shape[1]), lambda b, s: (b * ns + s, 0)),
                jax.ShapeDtypeStruct((rows, w.shape[1]), BF16))

    win_in, win_out, win_shape = cast_specs(win)
    wout_in, wout_out, wout_shape = cast_specs(wout)
    return pl.pallas_call(
        functools.partial(_pool_kernel, tm=tm, gc=gc),
        grid=(B, ns),
        in_specs=[xspec, _const_spec((1, D)), _const_spec((1, D)),
                  _const_spec(wp.shape), _const_spec((1, D)), win_in, wout_in],
        out_specs=[xspec, win_out, wout_out],
        out_shape=[jax.ShapeDtypeStruct(x.shape, x.dtype), win_shape, wout_shape],
        scratch_shapes=[pltpu.VMEM((D // LANES, tm + SUBLANES * STAGE_PAD, LANES), F32),
                        pltpu.VMEM((tm, D), F32)]
                       + [pltpu.VMEM((w // 2 * SUBLANES, D - gi * gc), F32) for gi, w in enumerate(POOL_WINDOWS)],
        compiler_params=_params(),
        name="pool_layer",
    )(x, gpre, gpost, wp, scale, win, wout)


def _gelu_gate(gate, val):
    inner = gate * (GELU_C + (GELU_C * 0.044715) * (gate * gate))
    return (gate * val) * (0.5 * jnp.tanh(inner) + 0.5)


def _ffn_kernel(x_ref, gpre_ref, gpost_ref, win_ref, cw_ref, cb_ref, wout_ref, o_ref,
                stage_in, stage_out, h_scr, act_scr, halo, *, tm, fc, nc):
    F = nc * fc
    D = wout_ref.shape[1]
    sub_rows = tm // FFN_SUBTILES
    Q = sub_rows // SUBLANES
    P = Q + STAGE_PAD
    nslab = D // LANES

    @pl.when(pl.program_id(1) == 0)
    def _():
        halo[...] = jnp.zeros(halo.shape, F32)

    sub = lax.broadcasted_iota(jnp.int32, (SUBLANES, fc), 0)

    def rows(t):
        return slice(t * sub_rows, (t + 1) * sub_rows)

    def pre_norm(t):
        h = _rms(x_ref[0, rows(t), :], gpre_ref[...])
        for j in range(nslab):
            for i in range(SUBLANES):
                stage_in[t, j, i * P:i * P + Q, :] = h[i * Q:(i + 1) * Q, j * LANES:(j + 1) * LANES]
        for k in range(0, Q, 2):
            blk = [jnp.concatenate([stage_in[t, j, pl.ds(k + d, SUBLANES, stride=P), :] for j in range(nslab)], axis=1)
                   for d in range(2)]
            h_scr[t, k * SUBLANES:(k + 2) * SUBLANES, :] = jnp.concatenate(blk, axis=0).astype(BF16)

    def conv(t, col):
        u = jnp.dot(h_scr[t], win_ref[:, col:col + fc], preferred_element_type=F32)
        last1, last2 = u[sub_rows - SUBLANES:, :], u[sub_rows - 2 * SUBLANES:sub_rows - SUBLANES, :]
        b1 = pltpu.roll(jnp.where(sub == SUBLANES - 1, halo[0:SUBLANES, col:col + fc], last1), 1, 0)
        b2 = pltpu.roll(jnp.where(sub == SUBLANES - 1, halo[SUBLANES:2 * SUBLANES, col:col + fc], last2), 1, 0)
        halo[0:SUBLANES, col:col + fc] = last1
        halo[SUBLANES:2 * SUBLANES, col:col + fc] = last2
        um1 = jnp.concatenate([b1, u[:sub_rows - SUBLANES, :]], axis=0)
        um2 = jnp.concatenate([b2, b1, u[:sub_rows - 2 * SUBLANES, :]], axis=0)
        return (cw_ref[0:1, col:col + fc] * um2 + cw_ref[1:2, col:col + fc] * um1
                + cw_ref[2:3, col:col + fc] * u + cb_ref[:, col:col + fc])

    def up(t):
        for c in range(nc):
            gate = conv(t, c * fc)
            val = conv(t, F + c * fc)
            act_scr[t, :, c * fc:(c + 1) * fc] = _gelu_gate(gate, val).astype(BF16)

    def down(t):
        f = jnp.dot(act_scr[t], wout_ref[...], preferred_element_type=F32)
        for k in range(Q):
            for j in range(nslab):
                stage_out[t, j, pl.ds(k, SUBLANES, stride=P), :] = f[k * SUBLANES:(k + 1) * SUBLANES, j * LANES:(j + 1) * LANES]

    def post_norm(t):
        fn = jnp.concatenate([jnp.concatenate([stage_out[t, j, i * P:i * P + Q, :] for i in range(SUBLANES)], axis=0)
                              for j in range(nslab)], axis=1)
        o_ref[0, rows(t), :] = x_ref[0, rows(t), :] + _rms(fn, gpost_ref[...])

    pre_norm(0)
    for t in range(FFN_SUBTILES):
        up(t)
        if t + 1 < FFN_SUBTILES:
            pre_norm(t + 1)
        if t >= 1:
            post_norm(t - 1)
        down(t)
    post_norm(FFN_SUBTILES - 1)


def _ffn_layer(x, gpre, gpost, win, cw, cb, wout, *, layer, tm):
    B, S, D = x.shape
    F = wout.shape[1]
    fc = FFN_CHUNK
    nc = F // fc
    sub_rows = tm // FFN_SUBTILES
    assert cw.shape == (CONV_WIDTH, 2 * F) and sub_rows % (2 * SUBLANES) == 0
    xspec = pl.BlockSpec((1, tm, D), lambda b, s: (b, s, 0))
    stage = pltpu.VMEM((FFN_SUBTILES, D // LANES, sub_rows + SUBLANES * STAGE_PAD, LANES), F32)
    return pl.pallas_call(
        functools.partial(_ffn_kernel, tm=tm, fc=fc, nc=nc),
        grid=(B, S // tm),
        in_specs=[xspec, _const_spec((1, D)), _const_spec((1, D)), _layer_spec(win.shape, layer),
                  _const_spec(cw.shape), _const_spec(cb.shape), _layer_spec(wout.shape, layer)],
        out_specs=xspec,
        out_shape=jax.ShapeDtypeStruct(x.shape, x.dtype),
        scratch_shapes=[stage, stage,
                        pltpu.VMEM((FFN_SUBTILES, sub_rows, D), BF16), pltpu.VMEM((FFN_SUBTILES, sub_rows, F), BF16),
                        pltpu.VMEM(((CONV_WIDTH - 1) * SUBLANES, 2 * F), F32)],
        compiler_params=_params(),
        name="ffn_layer",
    )(x, gpre, gpost, win, cw, cb, wout)


def _rope_tables(pos, invf, sign, tm):
    ngrp = LANES // HALF
    q4 = tm // ngrp
    grp = lax.broadcasted_iota(jnp.int32, (q4, LANES), 1) // HALF
    packed = jnp.zeros((q4, LANES), F32)
    for g in range(ngrp):
        packed = jnp.where(grp == g, pos[g * q4:(g + 1) * q4, :], packed)
    ang = packed * invf
    tables = []
    for table in (jnp.cos(ang), jnp.sin(ang)):
        rolled = [table] + [pltpu.roll(table, HALF * d, 1) for d in range(1, ngrp)]
        quarters = []
        for g in range(ngrp):
            out = rolled[(0 - g) % ngrp]
            for h in range(1, ngrp):
                out = jnp.where(grp == h, rolled[(h - g) % ngrp], out)
            quarters.append(out)
        tables.append(jnp.concatenate(quarters, axis=0))
    return tables[0], tables[1] * sign


def _rope(t, cos, sin):
    return t * cos + pltpu.roll(t, LANES // 2, 1) * sin


def _kv_kernel(x_ref, g_ref, pos_ref, invf_ref, sign_ref, wk_ref, wv_ref, k_ref, v_ref, cos_ref, sin_ref, *, tm):
    cos, sin = _rope_tables(pos_ref[0].astype(F32), invf_ref[...], sign_ref[...], tm)
    cos_ref[0] = cos
    sin_ref[0] = sin
    h = _rms(x_ref[0], g_ref[...]).astype(BF16)
    kd = jnp.dot(h, wk_ref[...], preferred_element_type=F32)
    for j in range(N_KV_HEADS):
        k_ref[0, :, j * LANES:(j + 1) * LANES] = _rope(kd[:, j * LANES:(j + 1) * LANES], cos, sin).astype(BF16)
    v_ref[0] = jnp.dot(h, wv_ref[...], preferred_element_type=F32).astype(BF16)


def _kv_proj(x, g, positions, wk, wv, *, tm):
    B, S, D = x.shape
    W = N_KV_HEADS * LANES
    inv_freq = 1.0 / (ROPE_THETA ** (jnp.arange(0, HEAD_DIM, 2, dtype=F32) / HEAD_DIM))
    invf = jnp.tile(inv_freq, LANES // HALF)[None, :]
    sign = jnp.where(jnp.arange(LANES) < LANES // 2, -1.0, 1.0).astype(F32)[None, :]
    tspec = pl.BlockSpec((1, tm, LANES), lambda b, s: (b, s, 0))
    ospec = pl.BlockSpec((1, tm, W), lambda b, s: (b, s, 0))
    return pl.pallas_call(
        functools.partial(_kv_kernel, tm=tm),
        grid=(B, S // tm),
        in_specs=[pl.BlockSpec((1, tm, D), lambda b, s: (b, s, 0)), _const_spec((1, D)),
                  pl.BlockSpec((1, tm, 1), lambda b, s: (b, s, 0)), _const_spec((1, LANES)), _const_spec((1, LANES)),
                  _const_spec(wk.shape), _const_spec(wv.shape)],
        out_specs=[ospec, ospec, tspec, tspec],
        out_shape=[jax.ShapeDtypeStruct((B, S, W), BF16)] * 2 + [jax.ShapeDtypeStruct((B, S, LANES), F32)] * 2,
        compiler_params=_params(),
        name="kv_proj",
    )(x, g, positions[:, :, None], invf, sign, wk, wv)


def _attn_kernel(sinks_ref, x_ref, gpre_ref, gpost_ref, cos_ref, sin_ref, kc_ref, kp_ref, vc_ref, vp_ref,
                 wq_ref, wo_ref, o_ref, qa, qb, kbuf, vbuf, o_scr, *, tm):
    s_idx = pl.program_id(1)
    nblk = tm // BLOCK
    D = wq_ref.shape[0]
    cw = D // N_KV_HEADS
    lane = lax.broadcasted_iota(jnp.int32, (1, LANES), 1)
    first_of_pair = (lane % HEAD_DIM) < HALF

    def pre_norm(r0):
        return _rms(x_ref[0, pl.ds(r0, BLOCK), :], gpre_ref[...]).astype(BF16)

    def q_piece(h, j, r0, qdst):
        q = jnp.dot(h, wq_ref[:, j * cw:(j + 1) * cw], preferred_element_type=F32)
        cos, sin = cos_ref[0, pl.ds(r0, BLOCK), :], sin_ref[0, pl.ds(r0, BLOCK), :]
        for p in range(cw // LANES):
            qr = _rope(q[:, p * LANES:(p + 1) * LANES], cos, sin) * (ATTN_SCALE * LOG2E)
            qdst[Q_PER_KV * j + 2 * p] = jnp.where(first_of_pair, qr, 0.0).astype(BF16)
            qdst[Q_PER_KV * j + 2 * p + 1] = jnp.where(first_of_pair, 0.0, qr).astype(BF16)

    def post_norm(mo_parts, r0):
        mo = jnp.concatenate(mo_parts, axis=-1)
        o_ref[0, pl.ds(r0, BLOCK), :] = x_ref[0, pl.ds(r0, BLOCK), :] + _rms(mo, gpost_ref[...])

    kbuf[0:BLOCK, :] = kp_ref[0]
    kbuf[BLOCK:BLOCK + tm, :] = kc_ref[0]
    for j in range(N_KV_HEADS):
        vbuf[0:BLOCK, 2 * j * LANES:(2 * j + 1) * LANES] = vp_ref[0, :, j * LANES:(j + 1) * LANES]
        vbuf[BLOCK:BLOCK + tm, 2 * j * LANES:(2 * j + 1) * LANES] = vc_ref[0, :, j * LANES:(j + 1) * LANES]
        vbuf[:, (2 * j + 1) * LANES:(2 * j + 2) * LANES] = jnp.ones((tm + BLOCK, LANES), BF16)

    qi = lax.broadcasted_iota(jnp.int32, (BLOCK, 2 * BLOCK), 0)
    kj = lax.broadcasted_iota(jnp.int32, (BLOCK, 2 * BLOCK), 1)
    rel = BLOCK + qi - kj
    band = (rel >= 0) & (rel < WINDOW)
    out_lo = lax.broadcasted_iota(jnp.int32, (BLOCK, LANES), 1) < HEAD_DIM

    def block_step(jb, qcur, qnext, has_prev=True, has_next=True):
        r0 = pl.multiple_of(jb * BLOCK, BLOCK)
        rp = pl.multiple_of((jb - 1) * BLOCK, BLOCK) if has_prev else None
        rn = pl.multiple_of((jb + 1) * BLOCK, BLOCK) if has_next else None
        o_prev = o_scr[pl.ds(rp, BLOCK), :] if has_prev else None
        h_next = pre_norm(rn) if has_next else None
        seq_start = (s_idx * tm + r0 == 0).astype(jnp.int32)
        valid = band & (kj >= BLOCK * seq_start)
        mo_parts = []

        def scores(j):
            qblk = jnp.concatenate([qcur[Q_PER_KV * j + i] for i in range(Q_PER_KV)], axis=0)
            kblk = kbuf[pl.ds(r0, 2 * BLOCK), j * LANES:(j + 1) * LANES]
            return lax.dot_general(qblk, kblk, (((1,), (1,)), ((), ())), preferred_element_type=F32)

        def projections(j):
            if has_prev:
                mo_parts.append(jnp.dot(o_prev, wo_ref[:, j * cw:(j + 1) * cw], preferred_element_type=F32))
            if has_next:
                q_piece(h_next, j, rn, qnext)

        def values(j, sc):
            ps, sink_p = [], []
            for i in range(Q_PER_KV):
                sink = sinks_ref[Q_PER_KV * j + i] * LOG2E
                sh = jnp.where(valid, sc[i * BLOCK:(i + 1) * BLOCK], NEG_INF)
                m = jnp.maximum(jnp.max(sh, axis=-1, keepdims=True), sink)
                ps.append(jnp.exp2(sh - m).astype(BF16))
                sink_p.append(jnp.exp2(sink - m))
            vblk = vbuf[pl.ds(r0, 2 * BLOCK), 2 * j * LANES:(2 * j + 2) * LANES]
            ov = jnp.dot(jnp.concatenate(ps, axis=0), vblk, preferred_element_type=F32)
            heads = []
            for i in range(Q_PER_KV):
                num = ov[i * BLOCK:(i + 1) * BLOCK, :LANES]
                den = ov[i * BLOCK:(i + 1) * BLOCK, LANES:] + sink_p[i]
                heads.append(num / den)
            for pp in range(Q_PER_KV // 2):
                col = (Q_PER_KV // 2 * j + pp) * LANES
                o_scr[pl.ds(r0, BLOCK), col:col + LANES] = jnp.where(out_lo, heads[2 * pp], heads[2 * pp + 1]).astype(BF16)

        sc = scores(0)
        for j in range(N_KV_HEADS):
            sc_next = scores(j + 1) if j + 1 < N_KV_HEADS else None
            projections(j)
            values(j, sc)
            sc = sc_next
        if has_prev:
            post_norm(mo_parts, rp)

    h0 = pre_norm(0)
    for j in range(N_KV_HEADS):
        q_piece(h0, j, 0, qa)

    block_step(0, qa, qb, has_prev=False)

    def pair_body(i, carry):
        block_step(2 * i + 1, qb, qa)
        block_step(2 * i + 2, qa, qb)
        return carry

    lax.fori_loop(0, nblk // 2 - 1, pair_body, 0)
    block_step(nblk - 1, qb, qa, has_next=False)
    r_last = (nblk - 1) * BLOCK
    mo_last = jnp.dot(o_scr[r_last:r_last + BLOCK, :], wo_ref[...], preferred_element_type=F32)
    o_ref[0, r_last:r_last + BLOCK, :] = x_ref[0, r_last:r_last + BLOCK, :] + _rms(mo_last, gpost_ref[...])


def _attn_layer(x, gpre, gpost, cos, sin, k, v, wq, wo, sinks, *, layer, tm):
    B, S, D = x.shape
    W = N_KV_HEADS * LANES
    nbt = tm // BLOCK
    xspec = pl.BlockSpec((1, tm, D), lambda b, s: (b, s, 0))
    tspec = pl.BlockSpec((1, tm, LANES), lambda b, s: (b, s, 0))
    cur = pl.BlockSpec((1, tm, W), lambda b, s: (b, s, 0))
    prev = pl.BlockSpec((1, BLOCK, W), lambda b, s: (b, jnp.maximum(s * nbt - 1, 0), 0))
    return pl.pallas_call(
        functools.partial(_attn_kernel, tm=tm),
        grid=(B, S // tm),
        in_specs=[pl.BlockSpec(memory_space=pltpu.SMEM), xspec, _const_spec((1, D)), _const_spec((1, D)),
                  tspec, tspec, cur, prev, cur, prev, _layer_spec(wq.shape, layer), _layer_spec(wo.shape, layer)],
        out_specs=xspec,
        out_shape=jax.ShapeDtypeStruct(x.shape, x.dtype),
        scratch_shapes=[pltpu.VMEM((N_Q_HEADS, BLOCK, LANES), BF16), pltpu.VMEM((N_Q_HEADS, BLOCK, LANES), BF16),
                        pltpu.VMEM((tm + BLOCK, W), BF16), pltpu.VMEM((tm + BLOCK, 2 * W), BF16),
                        pltpu.VMEM((tm, D), BF16)],
        compiler_params=_params(),
        name="attn_layer",
    )(sinks, x, gpre, gpost, cos, sin, k, k, v, v, wq, wo)


def _pair_layout_q(wq):
    lead = wq.shape[:-1]
    w = wq.reshape(*lead, N_Q_HEADS // 2, 2, 2, HALF)
    w = jnp.swapaxes(w, -3, -2)
    return w.reshape(*lead, N_Q_HEADS * HEAD_DIM)


def _dup_layout_k(wk):
    D = wk.shape[0]
    w = wk.reshape(D, N_KV_HEADS, 2, 1, HALF)
    w = jnp.broadcast_to(w, (D, N_KV_HEADS, 2, 2, HALF))
    return w.reshape(D, N_KV_HEADS * LANES)


def _dup_layout_v(wv):
    D = wv.shape[0]
    w = wv.reshape(D, N_KV_HEADS, 1, HEAD_DIM)
    w = jnp.broadcast_to(w, (D, N_KV_HEADS, 2, HEAD_DIM))
    return w.reshape(D, N_KV_HEADS * LANES)


def kernel(x, positions, mix_pre_g, mix_post_g, pool_w, pool_scale, kv_norm_g, w_kv, w_q, w_o, sinks,
           ffn_pre_g, ffn_post_g, ffn_w_in, ffn_conv_w, ffn_conv_b, ffn_w_out):
    B, S, D = x.shape
    tm = min(TILE_ROWS, S)
    tm_ffn = min(FFN_TILE_ROWS, S)
    tm_attn = min(ATTN_TILE_ROWS, S)
    depth = mix_pre_g.shape[0]
    n_a = pool_w.shape[0]
    F = ffn_w_out.shape[1]
    hkvd = N_KV_HEADS * HEAD_DIM
    assert S % tm == 0 and S % tm_ffn == 0 and S % tm_attn == 0 and tm_attn % (2 * BLOCK) == 0 and F % FFN_CHUNK == 0

    assert n_a >= 1 and depth % n_a == 0
    per = depth // n_a
    win2d, wout2d = ffn_w_in.reshape(depth * D, 2 * F), ffn_w_out.reshape(depth * F, D)
    win_bf, wout_bf = [], []
    wq_all, wo_all = _pair_layout_q(w_q).astype(BF16), w_o.astype(BF16)
    cos = sin = k = v = None
    for layer in range(depth):
        gpre, gpost = mix_pre_g[layer][None, :], mix_post_g[layer][None, :]
        if layer < n_a:
            x, wi, wo_ = _pool_layer(x, gpre, gpost, pool_w[layer].astype(BF16), pool_scale[layer][None, :],
                                     win2d, wout2d, layer, n_a, tm=tm)
            win_bf.append(wi.reshape(per, D, 2 * F))
            wout_bf.append(wo_.reshape(per, F, D))
        else:
            if layer == n_a:
                k, v, cos, sin = _kv_proj(x, kv_norm_g[None, :], positions,
                                          _dup_layout_k(w_kv[:, :hkvd]).astype(BF16),
                                          _dup_layout_v(w_kv[:, hkvd:]).astype(BF16), tm=tm)
            j = layer - n_a
            x = _attn_layer(x, gpre, gpost, cos, sin, k, v, wq_all, wo_all, sinks[j], layer=j, tm=tm_attn)
        x = _ffn_layer(x, ffn_pre_g[layer][None, :], ffn_post_g[layer][None, :],
                       win_bf[layer // per], ffn_conv_w[layer], ffn_conv_b[layer][None, :], wout_bf[layer // per],
                       layer=layer % per, tm=tm_ffn)
    return x
```

```python
import functools
import math

import jax
import jax.numpy as jnp
from jax import lax
from jax.experimental import pallas as pl
from jax.experimental.pallas import tpu as pltpu

F32 = jnp.float32
BF16 = jnp.bfloat16

RMS_EPS = 1e-6
POOL_WINDOWS = (2, 4, 8, 16)
HEAD_DIM = 64
HALF = HEAD_DIM // 2
N_Q_HEADS = 16
N_KV_HEADS = 4
Q_PER_KV = N_Q_HEADS // N_KV_HEADS
WINDOW = 128
BLOCK = 128
ROPE_THETA = 10000.0
ATTN_SCALE = 1.0 / math.sqrt(HEAD_DIM)
NEG_INF = -1e30
CONV_WIDTH = 3
LANES = 128
SUBLANES = 8
STAGE_PAD = 8
FFN_CHUNK = 256
FFN_SUBTILES = 2
VMEM_LIMIT = 56 * 1024 * 1024

TILE_ROWS = 1024

GELU_C = math.sqrt(2.0 / math.pi)
LOG2E = math.log2(math.e)


def _rms(x, g):
    ms = jnp.mean(x * x, axis=-1, keepdims=True)
    return x * lax.rsqrt(ms + RMS_EPS) * g


def _const_spec(shape):
    nd = len(shape)
    return pl.BlockSpec(shape, lambda b, s: (0,) * nd, pipeline_mode=pl.Buffered(1))


def _layer_spec(shape, layer):
    nd = len(shape) - 1
    return pl.BlockSpec((None,) + tuple(shape[1:]), lambda b, s: (layer,) + (0,) * nd, pipeline_mode=pl.Buffered(1))


def _params():
    return pltpu.CompilerParams(dimension_semantics=("arbitrary", "arbitrary"),
                                vmem_limit_bytes=VMEM_LIMIT)


def _pool_kernel(x_ref, gpre_ref, gpost_ref, wp_ref, scale_ref, win_ref, wout_ref, o_ref, win_bf_ref, wout_bf_ref,
                 stage, hp, *hprev, tm, gc):
    win_bf_ref[...] = win_ref[...].astype(BF16)
    wout_bf_ref[...] = wout_ref[...].astype(BF16)
    s_idx = pl.program_id(1)
    D = hp.shape[1]
    Q = tm // SUBLANES
    P = Q + STAGE_PAD
    nslab = D // LANES

    @pl.when(s_idx == 0)
    def _():
        for prev in hprev:
            prev[...] = jnp.zeros(prev.shape, F32)

    x = x_ref[0]
    h = _rms(x, gpre_ref[...])
    for j in range(nslab):
        for i in range(SUBLANES):
            stage[j, i * P:i * P + Q, :] = h[i * Q:(i + 1) * Q, j * LANES:(j + 1) * LANES]
    for k in range(Q):
        hp[k * SUBLANES:(k + 1) * SUBLANES, :] = jnp.concatenate(
            [stage[j, pl.ds(k, SUBLANES, stride=P), :] for j in range(nslab)], axis=1)

    def shifted(t, sft, prev):
        rows, cols = sft * SUBLANES, t.shape[1]
        last = t[tm - rows:, :]
        sub = lax.broadcasted_iota(jnp.int32, (rows, cols), 0) % SUBLANES
        wrap = jnp.where(sub == SUBLANES - 1, prev[...], last)
        wrap = jnp.concatenate([pltpu.roll(wrap[m * SUBLANES:(m + 1) * SUBLANES, :], 1, 0) for m in range(sft)], axis=0)
        prev[...] = last
        return jnp.concatenate([wrap, t[:tm - rows, :]], axis=0)

    level, sums = hp[...], []
    for gi, (w, prev) in enumerate(zip(POOL_WINDOWS, hprev)):
        assert w == 2 ** (gi + 1)
        level = level + shifted(level, w // 2, prev)
        sums.append(level[:, :gc])
        level = level[:, gc:]

    first = (s_idx == 0).astype(F32)
    sub8 = lax.broadcasted_iota(jnp.int32, (SUBLANES, gc), 0)
    ys = []
    for gi, w in enumerate(POOL_WINDOWS):
        own = hp[:, gi * gc:(gi + 1) * gc]
        mean = sums[gi] * (1.0 / w)
        fix = [jnp.where(sub8 == 0, 1.0 + first * (w / (k + 1.0) - 1.0), 1.0) for k in range(w - 1)]
        head = jnp.concatenate(fix, axis=0) * mean[0:(w - 1) * SUBLANES, :]
        mean = jnp.concatenate([head, mean[(w - 1) * SUBLANES:, :]], axis=0)
        d = mean - own
        ys.append(jnp.dot(d.astype(BF16), wp_ref[gi], preferred_element_type=F32))
    m = jnp.concatenate(ys, axis=-1) * scale_ref[...]
    r = _rms(m, gpost_ref[...])
    for k in range(Q):
        for j in range(nslab):
            stage[j, pl.ds(k, SUBLANES, stride=P), :] = r[k * SUBLANES:(k + 1) * SUBLANES, j * LANES:(j + 1) * LANES]
    rn = jnp.concatenate([jnp.concatenate([stage[j, i * P:i * P + Q, :] for i in range(SUBLANES)], axis=0)
                          for j in range(nslab)], axis=1)
    o_ref[0] = x + rn


def _pool_layer(x, gpre, gpost, wp, scale, win, wout, part, nparts, *, tm):
    B, S, D = x.shape
    gc = D // len(POOL_WINDOWS)
    wmax = max(POOL_WINDOWS)
    assert tm // SUBLANES >= wmax
    ns = S // tm
    nsteps = B * ns
    xspec = pl.BlockSpec((1, tm, D), lambda b, s: (b, s, 0))

    def cast_specs(w):
        rows = w.shape[0] // nparts
        blk = rows // nsteps
        assert w.shape[0] % nparts == 0 and rows % nsteps == 0 and blk % (2 * SUBLANES) == 0
        return (pl.BlockSpec((blk, w.shape[1]), lambda b, s: (part * nsteps + b * ns + s, 0)),
                pl.BlockSpec((blk, w.shape[1]), lambda b, s: (b * ns + s, 0)),
                jax.ShapeDtypeStruct((rows, w.shape[1]), BF16))

    win_in, win_out, win_shape = cast_specs(win)
    wout_in, wout_out, wout_shape = cast_specs(wout)
    return pl.pallas_call(
        functools.partial(_pool_kernel, tm=tm, gc=gc),
        grid=(B, ns),
        in_specs=[xspec, _const_spec((1, D)), _const_spec((1, D)),
                  _const_spec(wp.shape), _const_spec((1, D)), win_in, wout_in],
        out_specs=[xspec, win_out, wout_out],
        out_shape=[jax.ShapeDtypeStruct(x.shape, x.dtype), win_shape, wout_shape],
        scratch_shapes=[pltpu.VMEM((D // LANES, tm + SUBLANES * STAGE_PAD, LANES), F32),
                        pltpu.VMEM((tm, D), F32)]
                       + [pltpu.VMEM((w // 2 * SUBLANES, D - gi * gc), F32) for gi, w in enumerate(POOL_WINDOWS)],
        compiler_params=_params(),
        name="pool_layer",
    )(x, gpre, gpost, wp, scale, win, wout)


def _gelu_gate(gate, val):
    inner = gate * (GELU_C + (GELU_C * 0.044715) * (gate * gate))
    return (gate * val) * (0.5 * jnp.tanh(inner) + 0.5)


def _ffn_kernel(x_ref, gpre_ref, gpost_ref, win_ref, cw_ref, cb_ref, wout_ref, o_ref,
                stage_in, stage_out, h_scr, act_scr, halo, *, tm, fc, nc):
    F = nc * fc
    D = wout_ref.shape[1]
    sub_rows = tm // FFN_SUBTILES
    Q = sub_rows // SUBLANES
    P = Q + STAGE_PAD
    nslab = D // LANES

    @pl.when(pl.program_id(1) == 0)
    def _():
        halo[...] = jnp.zeros(halo.shape, F32)

    sub = lax.broadcasted_iota(jnp.int32, (SUBLANES, fc), 0)

    def rows(t):
        return slice(t * sub_rows, (t + 1) * sub_rows)

    def pre_norm(t):
        h = _rms(x_ref[0, rows(t), :], gpre_ref[...])
        for j in range(nslab):
            for i in range(SUBLANES):
                stage_in[t, j, i * P:i * P + Q, :] = h[i * Q:(i + 1) * Q, j * LANES:(j + 1) * LANES]
        for k in range(0, Q, 2):
            blk = [jnp.concatenate([stage_in[t, j, pl.ds(k + d, SUBLANES, stride=P), :] for j in range(nslab)], axis=1)
                   for d in range(2)]
            h_scr[t, k * SUBLANES:(k + 2) * SUBLANES, :] = jnp.concatenate(blk, axis=0).astype(BF16)

    def conv(t, col):
        u = jnp.dot(h_scr[t], win_ref[:, col:col + fc], preferred_element_type=F32)
        last1, last2 = u[sub_rows - SUBLANES:, :], u[sub_rows - 2 * SUBLANES:sub_rows - SUBLANES, :]
        b1 = pltpu.roll(jnp.where(sub == SUBLANES - 1, halo[0:SUBLANES, col:col + fc], last1), 1, 0)
        b2 = pltpu.roll(jnp.where(sub == SUBLANES - 1, halo[SUBLANES:2 * SUBLANES, col:col + fc], last2), 1, 0)
        halo[0:SUBLANES, col:col + fc] = last1
        halo[SUBLANES:2 * SUBLANES, col:col + fc] = last2
        um1 = jnp.concatenate([b1, u[:sub_rows - SUBLANES, :]], axis=0)
        um2 = jnp.concatenate([b2, b1, u[:sub_rows - 2 * SUBLANES, :]], axis=0)
        return (cw_ref[0:1, col:col + fc] * um2 + cw_ref[1:2, col:col + fc] * um1
                + cw_ref[2:3, col:col + fc] * u + cb_ref[:, col:col + fc])

    def up(t):
        for c in range(nc):
            gate = conv(t, c * fc)
            val = conv(t, F + c * fc)
            act_scr[t, :, c * fc:(c + 1) * fc] = _gelu_gate(gate, val).astype(BF16)

    def down(t):
        f = jnp.dot(act_scr[t], wout_ref[...], preferred_element_type=F32)
        for k in range(Q):
            for j in range(nslab):
                stage_out[t, j, pl.ds(k, SUBLANES, stride=P), :] = f[k * SUBLANES:(k + 1) * SUBLANES, j * LANES:(j + 1) * LANES]

    def post_norm(t):
        fn = jnp.concatenate([jnp.concatenate([stage_out[t, j, i * P:i * P + Q, :] for i in range(SUBLANES)], axis=0)
                              for j in range(nslab)], axis=1)
        o_ref[0, rows(t), :] = x_ref[0, rows(t), :] + _rms(fn, gpost_ref[...])

    pre_norm(0)
    for t in range(FFN_SUBTILES):
        up(t)
        if t + 1 < FFN_SUBTILES:
            pre_norm(t + 1)
        if t >= 1:
            post_norm(t - 1)
        down(t)
    post_norm(FFN_SUBTILES - 1)


def _ffn_layer(x, gpre, gpost, win, cw, cb, wout, *, layer, tm):
    B, S, D = x.shape
    F = wout.shape[1]
    fc = FFN_CHUNK
    nc = F // fc
    sub_rows = tm // FFN_SUBTILES
    assert cw.shape == (CONV_WIDTH, 2 * F) and sub_rows % (2 * SUBLANES) == 0
    xspec = pl.BlockSpec((1, tm, D), lambda b, s: (b, s, 0))
    stage = pltpu.VMEM((FFN_SUBTILES, D // LANES, sub_rows + SUBLANES * STAGE_PAD, LANES), F32)
    return pl.pallas_call(
        functools.partial(_ffn_kernel, tm=tm, fc=fc, nc=nc),
        grid=(B, S // tm),
        in_specs=[xspec, _const_spec((1, D)), _const_spec((1, D)), _layer_spec(win.shape, layer),
                  _const_spec(cw.shape), _const_spec(cb.shape), _layer_spec(wout.shape, layer)],
        out_specs=xspec,
        out_shape=jax.ShapeDtypeStruct(x.shape, x.dtype),
        scratch_shapes=[stage, stage,
                        pltpu.VMEM((FFN_SUBTILES, sub_rows, D), BF16), pltpu.VMEM((FFN_SUBTILES, sub_rows, F), BF16),
                        pltpu.VMEM(((CONV_WIDTH - 1) * SUBLANES, 2 * F), F32)],
        compiler_params=_params(),
        name="ffn_layer",
    )(x, gpre, gpost, win, cw, cb, wout)


def _rope_tables(pos, invf, sign, tm):
    ngrp = LANES // HALF
    q4 = tm // ngrp
    grp = lax.broadcasted_iota(jnp.int32, (q4, LANES), 1) // HALF
    packed = jnp.zeros((q4, LANES), F32)
    for g in range(ngrp):
        packed = jnp.where(grp == g, pos[g * q4:(g + 1) * q4, :], packed)
    ang = packed * invf
    tables = []
    for table in (jnp.cos(ang), jnp.sin(ang)):
        rolled = [table] + [pltpu.roll(table, HALF * d, 1) for d in range(1, ngrp)]
        quarters = []
        for g in range(ngrp):
            out = rolled[(0 - g) % ngrp]
            for h in range(1, ngrp):
                out = jnp.where(grp == h, rolled[(h - g) % ngrp], out)
            quarters.append(out)
        tables.append(jnp.concatenate(quarters, axis=0))
    return tables[0], tables[1] * sign


def _rope(t, cos, sin):
    return t * cos + pltpu.roll(t, LANES // 2, 1) * sin


def _kv_kernel(x_ref, g_ref, pos_ref, invf_ref, sign_ref, wk_ref, wv_ref, k_ref, v_ref, cos_ref, sin_ref, *, tm):
    cos, sin = _rope_tables(pos_ref[0].astype(F32), invf_ref[...], sign_ref[...], tm)
    cos_ref[0] = cos
    sin_ref[0] = sin
    h = _rms(x_ref[0], g_ref[...]).astype(BF16)
    kd = jnp.dot(h, wk_ref[...], preferred_element_type=F32)
    for j in range(N_KV_HEADS):
        k_ref[0, :, j * LANES:(j + 1) * LANES] = _rope(kd[:, j * LANES:(j + 1) * LANES], cos, sin).astype(BF16)
    v_ref[0] = jnp.dot(h, wv_ref[...], preferred_element_type=F32).astype(BF16)


def _kv_proj(x, g, positions, wk, wv, *, tm):
    B, S, D = x.shape
    W = N_KV_HEADS * LANES
    inv_freq = 1.0 / (ROPE_THETA ** (jnp.arange(0, HEAD_DIM, 2, dtype=F32) / HEAD_DIM))
    invf = jnp.tile(inv_freq, LANES // HALF)[None, :]
    sign = jnp.where(jnp.arange(LANES) < LANES // 2, -1.0, 1.0).astype(F32)[None, :]
    tspec = pl.BlockSpec((1, tm, LANES), lambda b, s: (b, s, 0))
    ospec = pl.BlockSpec((1, tm, W), lambda b, s: (b, s, 0))
    return pl.pallas_call(
        functools.partial(_kv_kernel, tm=tm),
        grid=(B, S // tm),
        in_specs=[pl.BlockSpec((1, tm, D), lambda b, s: (b, s, 0)), _const_spec((1, D)),
                  pl.BlockSpec((1, tm, 1), lambda b, s: (b, s, 0)), _const_spec((1, LANES)), _const_spec((1, LANES)),
                  _const_spec(wk.shape), _const_spec(wv.shape)],
        out_specs=[ospec, ospec, tspec, tspec],
        out_shape=[jax.ShapeDtypeStruct((B, S, W), BF16)] * 2 + [jax.ShapeDtypeStruct((B, S, LANES), F32)] * 2,
        compiler_params=_params(),
        name="kv_proj",
    )(x, g, positions[:, :, None], invf, sign, wk, wv)


def _attn_kernel(sinks_ref, x_ref, gpre_ref, gpost_ref, cos_ref, sin_ref, kc_ref, kp_ref, vc_ref, vp_ref,
                 wq_ref, wo_ref, o_ref, qa, qb, kbuf, vbuf, o_scr, *, tm):
    s_idx = pl.program_id(1)
    nblk = tm // BLOCK
    D = wq_ref.shape[0]
    cw = D // N_KV_HEADS
    lane = lax.broadcasted_iota(jnp.int32, (1, LANES), 1)
    first_of_pair = (lane % HEAD_DIM) < HALF

    def pre_norm(r0):
        return _rms(x_ref[0, pl.ds(r0, BLOCK), :], gpre_ref[...]).astype(BF16)

    def q_piece(h, j, r0, qdst):
        q = jnp.dot(h, wq_ref[:, j * cw:(j + 1) * cw], preferred_element_type=F32)
        cos, sin = cos_ref[0, pl.ds(r0, BLOCK), :], sin_ref[0, pl.ds(r0, BLOCK), :]
        for p in range(cw // LANES):
            qr = _rope(q[:, p * LANES:(p + 1) * LANES], cos, sin) * (ATTN_SCALE * LOG2E)
            qdst[Q_PER_KV * j + 2 * p] = jnp.where(first_of_pair, qr, 0.0).astype(BF16)
            qdst[Q_PER_KV * j + 2 * p + 1] = jnp.where(first_of_pair, 0.0, qr).astype(BF16)

    def post_norm(mo_parts, r0):
        mo = jnp.concatenate(mo_parts, axis=-1)
        o_ref[0, pl.ds(r0, BLOCK), :] = x_ref[0, pl.ds(r0, BLOCK), :] + _rms(mo, gpost_ref[...])

    kbuf[0:BLOCK, :] = kp_ref[0]
    kbuf[BLOCK:BLOCK + tm, :] = kc_ref[0]
    for j in range(N_KV_HEADS):
        vbuf[0:BLOCK, 2 * j * LANES:(2 * j + 1) * LANES] = vp_ref[0, :, j * LANES:(j + 1) * LANES]
        vbuf[BLOCK:BLOCK + tm, 2 * j * LANES:(2 * j + 1) * LANES] = vc_ref[0, :, j * LANES:(j + 1) * LANES]
        vbuf[:, (2 * j + 1) * LANES:(2 * j + 2) * LANES] = jnp.ones((tm + BLOCK, LANES), BF16)

    qi = lax.broadcasted_iota(jnp.int32, (BLOCK, 2 * BLOCK), 0)
    kj = lax.broadcasted_iota(jnp.int32, (BLOCK, 2 * BLOCK), 1)
    rel = BLOCK + qi - kj
    band = (rel >= 0) & (rel < WINDOW)
    out_lo = lax.broadcasted_iota(jnp.int32, (BLOCK, LANES), 1) < HEAD_DIM

    def block_step(jb, qcur, qnext, has_prev=True, has_next=True):
        r0 = pl.multiple_of(jb * BLOCK, BLOCK)
        rp = pl.multiple_of((jb - 1) * BLOCK, BLOCK) if has_prev else None
        rn = pl.multiple_of((jb + 1) * BLOCK, BLOCK) if has_next else None
        o_prev = o_scr[pl.ds(rp, BLOCK), :] if has_prev else None
        h_next = pre_norm(rn) if has_next else None
        seq_start = (s_idx * tm + r0 == 0).astype(jnp.int32)
        valid = band & (kj >= BLOCK * seq_start)
        mo_parts = []

        def scores(j):
            qblk = jnp.concatenate([qcur[Q_PER_KV * j + i] for i in range(Q_PER_KV)], axis=0)
            kblk = kbuf[pl.ds(r0, 2 * BLOCK), j * LANES:(j + 1) * LANES]
            return lax.dot_general(qblk, kblk, (((1,), (1,)), ((), ())), preferred_element_type=F32)

        def projections(j):
            if has_prev:
                mo_parts.append(jnp.dot(o_prev, wo_ref[:, j * cw:(j + 1) * cw], preferred_element_type=F32))
            if has_next:
                q_piece(h_next, j, rn, qnext)

        def values(j, sc):
            ps, sink_p = [], []
            for i in range(Q_PER_KV):
                sink = sinks_ref[Q_PER_KV * j + i] * LOG2E
                sh = jnp.where(valid, sc[i * BLOCK:(i + 1) * BLOCK], NEG_INF)
                m = jnp.maximum(jnp.max(sh, axis=-1, keepdims=True), sink)
                ps.append(jnp.exp2(sh - m).astype(BF16))
                sink_p.append(jnp.exp2(sink - m))
            vblk = vbuf[pl.ds(r0, 2 * BLOCK), 2 * j * LANES:(2 * j + 2) * LANES]
            ov = jnp.dot(jnp.concatenate(ps, axis=0), vblk, preferred_element_type=F32)
            heads = []
            for i in range(Q_PER_KV):
                num = ov[i * BLOCK:(i + 1) * BLOCK, :LANES]
                den = ov[i * BLOCK:(i + 1) * BLOCK, LANES:] + sink_p[i]
                heads.append(num / den)
            for pp in range(Q_PER_KV // 2):
                col = (Q_PER_KV // 2 * j + pp) * LANES
                o_scr[pl.ds(r0, BLOCK), col:col + LANES] = jnp.where(out_lo, heads[2 * pp], heads[2 * pp + 1]).astype(BF16)

        sc = scores(0)
        for j in range(N_KV_HEADS):
            sc_next = scores(j + 1) if j + 1 < N_KV_HEADS else None
            projections(j)
            values(j, sc)
            sc = sc_next
        if has_prev:
            post_norm(mo_parts, rp)

    h0 = pre_norm(0)
    for j in range(N_KV_HEADS):
        q_piece(h0, j, 0, qa)

    block_step(0, qa, qb, has_prev=False)

    def pair_body(i, carry):
        block_step(2 * i + 1, qb, qa)
        block_step(2 * i + 2, qa, qb)
        return carry

    lax.fori_loop(0, nblk // 2 - 1, pair_body, 0)
    block_step(nblk - 1, qb, qa, has_next=False)
    r_last = (nblk - 1) * BLOCK
    mo_last = jnp.dot(o_scr[r_last:r_last + BLOCK, :], wo_ref[...], preferred_element_type=F32)
    o_ref[0, r_last:r_last + BLOCK, :] = x_ref[0, r_last:r_last + BLOCK, :] + _rms(mo_last, gpost_ref[...])


def _attn_layer(x, gpre, gpost, cos, sin, k, v, wq, wo, sinks, *, layer, tm):
    B, S, D = x.shape
    W = N_KV_HEADS * LANES
    nbt = tm // BLOCK
    xspec = pl.BlockSpec((1, tm, D), lambda b, s: (b, s, 0))
    tspec = pl.BlockSpec((1, tm, LANES), lambda b, s: (b, s, 0))
    cur = pl.BlockSpec((1, tm, W), lambda b, s: (b, s, 0))
    prev = pl.BlockSpec((1, BLOCK, W), lambda b, s: (b, jnp.maximum(s * nbt - 1, 0), 0))
    return pl.pallas_call(
        functools.partial(_attn_kernel, tm=tm),
        grid=(B, S // tm),
        in_specs=[pl.BlockSpec(memory_space=pltpu.SMEM), xspec, _const_spec((1, D)), _const_spec((1, D)),
                  tspec, tspec, cur, prev, cur, prev, _layer_spec(wq.shape, layer), _layer_spec(wo.shape, layer)],
        out_specs=xspec,
        out_shape=jax.ShapeDtypeStruct(x.shape, x.dtype),
        scratch_shapes=[pltpu.VMEM((N_Q_HEADS, BLOCK, LANES), BF16), pltpu.VMEM((N_Q_HEADS, BLOCK, LANES), BF16),
                        pltpu.VMEM((tm + BLOCK, W), BF16), pltpu.VMEM((tm + BLOCK, 2 * W), BF16),
                        pltpu.VMEM((tm, D), BF16)],
        compiler_params=_params(),
        name="attn_layer",
    )(sinks, x, gpre, gpost, cos, sin, k, k, v, v, wq, wo)


def _pair_layout_q(wq):
    lead = wq.shape[:-1]
    w = wq.reshape(*lead, N_Q_HEADS // 2, 2, 2, HALF)
    w = jnp.swapaxes(w, -3, -2)
    return w.reshape(*lead, N_Q_HEADS * HEAD_DIM)


def _dup_layout_k(wk):
    D = wk.shape[0]
    w = wk.reshape(D, N_KV_HEADS, 2, 1, HALF)
    w = jnp.broadcast_to(w, (D, N_KV_HEADS, 2, 2, HALF))
    return w.reshape(D, N_KV_HEADS * LANES)


def _dup_layout_v(wv):
    D = wv.shape[0]
    w = wv.reshape(D, N_KV_HEADS, 1, HEAD_DIM)
    w = jnp.broadcast_to(w, (D, N_KV_HEADS, 2, HEAD_DIM))
    return w.reshape(D, N_KV_HEADS * LANES)


def kernel(x, positions, mix_pre_g, mix_post_g, pool_w, pool_scale, kv_norm_g, w_kv, w_q, w_o, sinks,
           ffn_pre_g, ffn_post_g, ffn_w_in, ffn_conv_w, ffn_conv_b, ffn_w_out):
    B, S, D = x.shape
    tm = min(TILE_ROWS, S)
    depth = mix_pre_g.shape[0]
    n_a = pool_w.shape[0]
    F = ffn_w_out.shape[1]
    hkvd = N_KV_HEADS * HEAD_DIM
    assert S % tm == 0 and tm % (2 * BLOCK) == 0 and F % FFN_CHUNK == 0

    assert n_a >= 1 and depth % n_a == 0
    per = depth // n_a
    win2d, wout2d = ffn_w_in.reshape(depth * D, 2 * F), ffn_w_out.reshape(depth * F, D)
    win_bf, wout_bf = [], []
    wq_all, wo_all = _pair_layout_q(w_q).astype(BF16), w_o.astype(BF16)
    cos = sin = k = v = None
    for layer in range(depth):
        gpre, gpost = mix_pre_g[layer][None, :], mix_post_g[layer][None, :]
        if layer < n_a:
            x, wi, wo_ = _pool_layer(x, gpre, gpost, pool_w[layer].astype(BF16), pool_scale[layer][None, :],
                                     win2d, wout2d, layer, n_a, tm=tm)
            win_bf.append(wi.reshape(per, D, 2 * F))
            wout_bf.append(wo_.reshape(per, F, D))
        else:
            if layer == n_a:
                k, v, cos, sin = _kv_proj(x, kv_norm_g[None, :], positions,
                                          _dup_layout_k(w_kv[:, :hkvd]).astype(BF16),
                                          _dup_layout_v(w_kv[:, hkvd:]).astype(BF16), tm=tm)
            j = layer - n_a
            x = _attn_layer(x, gpre, gpost, cos, sin, k, v, wq_all, wo_all, sinks[j], layer=j, tm=tm)
        x = _ffn_layer(x, ffn_pre_g[layer][None, :], ffn_post_g[layer][None, :],
                       win_bf[layer // per], ffn_conv_w[layer], ffn_conv_b[layer][None, :], wout_bf[layer // per],
                       layer=layer % per, tm=tm)
    return x
```

```python
import functools
import math

import jax
import jax.numpy as jnp
from jax import lax
from jax.experimental import pallas as pl
from jax.experimental.pallas import tpu as pltpu

F32 = jnp.float32
BF16 = jnp.bfloat16

RMS_EPS = 1e-6
POOL_WINDOWS = (2, 4, 8, 16)
HEAD_DIM = 64
HALF = HEAD_DIM // 2
N_Q_HEADS = 16
N_KV_HEADS = 4
Q_PER_KV = N_Q_HEADS // N_KV_HEADS
WINDOW = 128
BLOCK = 128
ROPE_THETA = 10000.0
ATTN_SCALE = 1.0 / math.sqrt(HEAD_DIM)
NEG_INF = -1e30
CONV_WIDTH = 3
LANES = 128
SUBLANES = 8
STAGE_PAD = 8
FFN_CHUNK = 256
VMEM_LIMIT = 56 * 1024 * 1024

TILE_ROWS = 1024

GELU_C = math.sqrt(2.0 / math.pi)
LOG2E = math.log2(math.e)


def _rms(x, g):
    ms = jnp.mean(x * x, axis=-1, keepdims=True)
    return x * lax.rsqrt(ms + RMS_EPS) * g


def _const_spec(shape):
    nd = len(shape)
    return pl.BlockSpec(shape, lambda b, s: (0,) * nd, pipeline_mode=pl.Buffered(1))


def _layer_spec(shape, layer):
    nd = len(shape) - 1
    return pl.BlockSpec((None,) + tuple(shape[1:]), lambda b, s: (layer,) + (0,) * nd, pipeline_mode=pl.Buffered(1))


def _params():
    return pltpu.CompilerParams(dimension_semantics=("arbitrary", "arbitrary"),
                                vmem_limit_bytes=VMEM_LIMIT)


def _pool_kernel(x_ref, gpre_ref, gpost_ref, wp_ref, scale_ref, win_ref, wout_ref, o_ref, win_bf_ref, wout_bf_ref,
                 stage, hp, *hprev, tm, gc):
    win_bf_ref[...] = win_ref[...].astype(BF16)
    wout_bf_ref[...] = wout_ref[...].astype(BF16)
    s_idx = pl.program_id(1)
    D = hp.shape[1]
    Q = tm // SUBLANES
    P = Q + STAGE_PAD
    nslab = D // LANES

    @pl.when(s_idx == 0)
    def _():
        for prev in hprev:
            prev[...] = jnp.zeros(prev.shape, F32)

    x = x_ref[0]
    h = _rms(x, gpre_ref[...])
    for j in range(nslab):
        for i in range(SUBLANES):
            stage[j, i * P:i * P + Q, :] = h[i * Q:(i + 1) * Q, j * LANES:(j + 1) * LANES]
    for k in range(Q):
        hp[k * SUBLANES:(k + 1) * SUBLANES, :] = jnp.concatenate(
            [stage[j, pl.ds(k, SUBLANES, stride=P), :] for j in range(nslab)], axis=1)

    def shifted(t, sft, prev):
        rows, cols = sft * SUBLANES, t.shape[1]
        last = t[tm - rows:, :]
        sub = lax.broadcasted_iota(jnp.int32, (rows, cols), 0) % SUBLANES
        wrap = jnp.where(sub == SUBLANES - 1, prev[...], last)
        wrap = jnp.concatenate([pltpu.roll(wrap[m * SUBLANES:(m + 1) * SUBLANES, :], 1, 0) for m in range(sft)], axis=0)
        prev[...] = last
        return jnp.concatenate([wrap, t[:tm - rows, :]], axis=0)

    level, sums = hp[...], []
    for gi, (w, prev) in enumerate(zip(POOL_WINDOWS, hprev)):
        assert w == 2 ** (gi + 1)
        level = level + shifted(level, w // 2, prev)
        sums.append(level[:, :gc])
        level = level[:, gc:]

    first = (s_idx == 0).astype(F32)
    sub8 = lax.broadcasted_iota(jnp.int32, (SUBLANES, gc), 0)
    ys = []
    for gi, w in enumerate(POOL_WINDOWS):
        own = hp[:, gi * gc:(gi + 1) * gc]
        mean = sums[gi] * (1.0 / w)
        fix = [jnp.where(sub8 == 0, 1.0 + first * (w / (k + 1.0) - 1.0), 1.0) for k in range(w - 1)]
        head = jnp.concatenate(fix, axis=0) * mean[0:(w - 1) * SUBLANES, :]
        mean = jnp.concatenate([head, mean[(w - 1) * SUBLANES:, :]], axis=0)
        d = mean - own
        ys.append(jnp.dot(d.astype(BF16), wp_ref[gi], preferred_element_type=F32))
    m = jnp.concatenate(ys, axis=-1) * scale_ref[...]
    r = _rms(m, gpost_ref[...])
    for k in range(Q):
        for j in range(nslab):
            stage[j, pl.ds(k, SUBLANES, stride=P), :] = r[k * SUBLANES:(k + 1) * SUBLANES, j * LANES:(j + 1) * LANES]
    rn = jnp.concatenate([jnp.concatenate([stage[j, i * P:i * P + Q, :] for i in range(SUBLANES)], axis=0)
                          for j in range(nslab)], axis=1)
    o_ref[0] = x + rn


def _pool_layer(x, gpre, gpost, wp, scale, win, wout, part, nparts, *, tm):
    B, S, D = x.shape
    gc = D // len(POOL_WINDOWS)
    wmax = max(POOL_WINDOWS)
    assert tm // SUBLANES >= wmax
    ns = S // tm
    nsteps = B * ns
    xspec = pl.BlockSpec((1, tm, D), lambda b, s: (b, s, 0))

    def cast_specs(w):
        rows = w.shape[0] // nparts
        blk = rows // nsteps
        assert w.shape[0] % nparts == 0 and rows % nsteps == 0 and blk % (2 * SUBLANES) == 0
        return (pl.BlockSpec((blk, w.shape[1]), lambda b, s: (part * nsteps + b * ns + s, 0)),
                pl.BlockSpec((blk, w.shape[1]), lambda b, s: (b * ns + s, 0)),
                jax.ShapeDtypeStruct((rows, w.shape[1]), BF16))

    win_in, win_out, win_shape = cast_specs(win)
    wout_in, wout_out, wout_shape = cast_specs(wout)
    return pl.pallas_call(
        functools.partial(_pool_kernel, tm=tm, gc=gc),
        grid=(B, ns),
        in_specs=[xspec, _const_spec((1, D)), _const_spec((1, D)),
                  _const_spec(wp.shape), _const_spec((1, D)), win_in, wout_in],
        out_specs=[xspec, win_out, wout_out],
        out_shape=[jax.ShapeDtypeStruct(x.shape, x.dtype), win_shape, wout_shape],
        scratch_shapes=[pltpu.VMEM((D // LANES, tm + SUBLANES * STAGE_PAD, LANES), F32),
                        pltpu.VMEM((tm, D), F32)]
                       + [pltpu.VMEM((w // 2 * SUBLANES, D - gi * gc), F32) for gi, w in enumerate(POOL_WINDOWS)],
        compiler_params=_params(),
        name="pool_layer",
    )(x, gpre, gpost, wp, scale, win, wout)


def _gelu_gate(gate, val):
    inner = gate * (GELU_C + (GELU_C * 0.044715) * (gate * gate))
    return (gate * val) * (0.5 * jnp.tanh(inner) + 0.5)


def _ffn_kernel(x_ref, gpre_ref, gpost_ref, win_ref, cw_ref, cb_ref, wout_ref, o_ref,
                stage_in, stage_out, h_scr, act_scr, halo, *, tm, fc, nc):
    F = nc * fc
    D = wout_ref.shape[1]
    Q = tm // SUBLANES
    P = Q + STAGE_PAD
    nslab = D // LANES

    @pl.when(pl.program_id(1) == 0)
    def _():
        halo[...] = jnp.zeros(halo.shape, F32)

    x = x_ref[0]
    h = _rms(x, gpre_ref[...])
    for j in range(nslab):
        for i in range(SUBLANES):
            stage_in[j, i * P:i * P + Q, :] = h[i * Q:(i + 1) * Q, j * LANES:(j + 1) * LANES]
    for k in range(0, Q, 2):
        blk = [jnp.concatenate([stage_in[j, pl.ds(k + d, SUBLANES, stride=P), :] for j in range(nslab)], axis=1)
               for d in range(2)]
        h_scr[k * SUBLANES:(k + 2) * SUBLANES, :] = jnp.concatenate(blk, axis=0).astype(BF16)

    sub = lax.broadcasted_iota(jnp.int32, (SUBLANES, fc), 0)

    def conv(col):
        u = jnp.dot(h_scr[...], win_ref[:, col:col + fc], preferred_element_type=F32)
        last1, last2 = u[tm - SUBLANES:, :], u[tm - 2 * SUBLANES:tm - SUBLANES, :]
        b1 = pltpu.roll(jnp.where(sub == SUBLANES - 1, halo[0:SUBLANES, col:col + fc], last1), 1, 0)
        b2 = pltpu.roll(jnp.where(sub == SUBLANES - 1, halo[SUBLANES:2 * SUBLANES, col:col + fc], last2), 1, 0)
        halo[0:SUBLANES, col:col + fc] = last1
        halo[SUBLANES:2 * SUBLANES, col:col + fc] = last2
        um1 = jnp.concatenate([b1, u[:tm - SUBLANES, :]], axis=0)
        um2 = jnp.concatenate([b2, b1, u[:tm - 2 * SUBLANES, :]], axis=0)
        return (cw_ref[0:1, col:col + fc] * um2 + cw_ref[1:2, col:col + fc] * um1
                + cw_ref[2:3, col:col + fc] * u + cb_ref[:, col:col + fc])

    for c in range(nc):
        gate = conv(c * fc)
        val = conv(F + c * fc)
        act_scr[:, c * fc:(c + 1) * fc] = _gelu_gate(gate, val).astype(BF16)

    f = jnp.dot(act_scr[...], wout_ref[...], preferred_element_type=F32)
    for k in range(Q):
        for j in range(nslab):
            stage_out[j, pl.ds(k, SUBLANES, stride=P), :] = f[k * SUBLANES:(k + 1) * SUBLANES, j * LANES:(j + 1) * LANES]
    fn = jnp.concatenate([jnp.concatenate([stage_out[j, i * P:i * P + Q, :] for i in range(SUBLANES)], axis=0)
                          for j in range(nslab)], axis=1)
    o_ref[0] = x + _rms(fn, gpost_ref[...])


def _ffn_layer(x, gpre, gpost, win, cw, cb, wout, *, layer, tm):
    B, S, D = x.shape
    F = wout.shape[1]
    fc = FFN_CHUNK
    nc = F // fc
    assert cw.shape == (CONV_WIDTH, 2 * F) and tm % (2 * SUBLANES) == 0
    xspec = pl.BlockSpec((1, tm, D), lambda b, s: (b, s, 0))
    stage = pltpu.VMEM((D // LANES, tm + SUBLANES * STAGE_PAD, LANES), F32)
    return pl.pallas_call(
        functools.partial(_ffn_kernel, tm=tm, fc=fc, nc=nc),
        grid=(B, S // tm),
        in_specs=[xspec, _const_spec((1, D)), _const_spec((1, D)), _layer_spec(win.shape, layer),
                  _const_spec(cw.shape), _const_spec(cb.shape), _layer_spec(wout.shape, layer)],
        out_specs=xspec,
        out_shape=jax.ShapeDtypeStruct(x.shape, x.dtype),
        scratch_shapes=[stage, stage,
                        pltpu.VMEM((tm, D), BF16), pltpu.VMEM((tm, F), BF16),
                        pltpu.VMEM(((CONV_WIDTH - 1) * SUBLANES, 2 * F), F32)],
        compiler_params=_params(),
        name="ffn_layer",
    )(x, gpre, gpost, win, cw, cb, wout)


def _rope_tables(pos, invf, sign, tm):
    ngrp = LANES // HALF
    q4 = tm // ngrp
    grp = lax.broadcasted_iota(jnp.int32, (q4, LANES), 1) // HALF
    packed = jnp.zeros((q4, LANES), F32)
    for g in range(ngrp):
        packed = jnp.where(grp == g, pos[g * q4:(g + 1) * q4, :], packed)
    ang = packed * invf
    tables = []
    for table in (jnp.cos(ang), jnp.sin(ang)):
        rolled = [table] + [pltpu.roll(table, HALF * d, 1) for d in range(1, ngrp)]
        quarters = []
        for g in range(ngrp):
            out = rolled[(0 - g) % ngrp]
            for h in range(1, ngrp):
                out = jnp.where(grp == h, rolled[(h - g) % ngrp], out)
            quarters.append(out)
        tables.append(jnp.concatenate(quarters, axis=0))
    return tables[0], tables[1] * sign


def _rope(t, cos, sin):
    return t * cos + pltpu.roll(t, LANES // 2, 1) * sin


def _kv_kernel(x_ref, g_ref, pos_ref, invf_ref, sign_ref, wk_ref, wv_ref, k_ref, v_ref, cos_ref, sin_ref, *, tm):
    cos, sin = _rope_tables(pos_ref[0].astype(F32), invf_ref[...], sign_ref[...], tm)
    cos_ref[0] = cos
    sin_ref[0] = sin
    h = _rms(x_ref[0], g_ref[...]).astype(BF16)
    kd = jnp.dot(h, wk_ref[...], preferred_element_type=F32)
    for j in range(N_KV_HEADS):
        k_ref[0, :, j * LANES:(j + 1) * LANES] = _rope(kd[:, j * LANES:(j + 1) * LANES], cos, sin).astype(BF16)
    v_ref[0] = jnp.dot(h, wv_ref[...], preferred_element_type=F32).astype(BF16)


def _kv_proj(x, g, positions, wk, wv, *, tm):
    B, S, D = x.shape
    W = N_KV_HEADS * LANES
    inv_freq = 1.0 / (ROPE_THETA ** (jnp.arange(0, HEAD_DIM, 2, dtype=F32) / HEAD_DIM))
    invf = jnp.tile(inv_freq, LANES // HALF)[None, :]
    sign = jnp.where(jnp.arange(LANES) < LANES // 2, -1.0, 1.0).astype(F32)[None, :]
    tspec = pl.BlockSpec((1, tm, LANES), lambda b, s: (b, s, 0))
    ospec = pl.BlockSpec((1, tm, W), lambda b, s: (b, s, 0))
    return pl.pallas_call(
        functools.partial(_kv_kernel, tm=tm),
        grid=(B, S // tm),
        in_specs=[pl.BlockSpec((1, tm, D), lambda b, s: (b, s, 0)), _const_spec((1, D)),
                  pl.BlockSpec((1, tm, 1), lambda b, s: (b, s, 0)), _const_spec((1, LANES)), _const_spec((1, LANES)),
                  _const_spec(wk.shape), _const_spec(wv.shape)],
        out_specs=[ospec, ospec, tspec, tspec],
        out_shape=[jax.ShapeDtypeStruct((B, S, W), BF16)] * 2 + [jax.ShapeDtypeStruct((B, S, LANES), F32)] * 2,
        compiler_params=_params(),
        name="kv_proj",
    )(x, g, positions[:, :, None], invf, sign, wk, wv)


def _attn_kernel(sinks_ref, x_ref, gpre_ref, gpost_ref, cos_ref, sin_ref, kc_ref, kp_ref, vc_ref, vp_ref,
                 wq_ref, wo_ref, o_ref, qa, qb, kbuf, vbuf, o_scr, *, tm):
    s_idx = pl.program_id(1)
    nblk = tm // BLOCK
    D = wq_ref.shape[0]
    cw = D // N_KV_HEADS
    lane = lax.broadcasted_iota(jnp.int32, (1, LANES), 1)
    first_of_pair = (lane % HEAD_DIM) < HALF

    def pre_norm(r0):
        return _rms(x_ref[0, pl.ds(r0, BLOCK), :], gpre_ref[...]).astype(BF16)

    def q_piece(h, j, r0, qdst):
        q = jnp.dot(h, wq_ref[:, j * cw:(j + 1) * cw], preferred_element_type=F32)
        cos, sin = cos_ref[0, pl.ds(r0, BLOCK), :], sin_ref[0, pl.ds(r0, BLOCK), :]
        for p in range(cw // LANES):
            qr = _rope(q[:, p * LANES:(p + 1) * LANES], cos, sin) * (ATTN_SCALE * LOG2E)
            qdst[Q_PER_KV * j + 2 * p] = jnp.where(first_of_pair, qr, 0.0).astype(BF16)
            qdst[Q_PER_KV * j + 2 * p + 1] = jnp.where(first_of_pair, 0.0, qr).astype(BF16)

    def post_norm(mo_parts, r0):
        mo = jnp.concatenate(mo_parts, axis=-1)
        o_ref[0, pl.ds(r0, BLOCK), :] = x_ref[0, pl.ds(r0, BLOCK), :] + _rms(mo, gpost_ref[...])

    kbuf[0:BLOCK, :] = kp_ref[0]
    kbuf[BLOCK:BLOCK + tm, :] = kc_ref[0]
    for j in range(N_KV_HEADS):
        vbuf[0:BLOCK, 2 * j * LANES:(2 * j + 1) * LANES] = vp_ref[0, :, j * LANES:(j + 1) * LANES]
        vbuf[BLOCK:BLOCK + tm, 2 * j * LANES:(2 * j + 1) * LANES] = vc_ref[0, :, j * LANES:(j + 1) * LANES]
        vbuf[:, (2 * j + 1) * LANES:(2 * j + 2) * LANES] = jnp.ones((tm + BLOCK, LANES), BF16)

    qi = lax.broadcasted_iota(jnp.int32, (BLOCK, 2 * BLOCK), 0)
    kj = lax.broadcasted_iota(jnp.int32, (BLOCK, 2 * BLOCK), 1)
    rel = BLOCK + qi - kj
    band = (rel >= 0) & (rel < WINDOW)
    out_lo = lax.broadcasted_iota(jnp.int32, (BLOCK, LANES), 1) < HEAD_DIM

    def block_step(jb, qcur, qnext, has_prev=True, has_next=True):
        r0 = pl.multiple_of(jb * BLOCK, BLOCK)
        rp = pl.multiple_of((jb - 1) * BLOCK, BLOCK) if has_prev else None
        rn = pl.multiple_of((jb + 1) * BLOCK, BLOCK) if has_next else None
        o_prev = o_scr[pl.ds(rp, BLOCK), :] if has_prev else None
        h_next = pre_norm(rn) if has_next else None
        seq_start = (s_idx * tm + r0 == 0).astype(jnp.int32)
        valid = band & (kj >= BLOCK * seq_start)
        mo_parts = []

        def scores(j):
            qblk = jnp.concatenate([qcur[Q_PER_KV * j + i] for i in range(Q_PER_KV)], axis=0)
            kblk = kbuf[pl.ds(r0, 2 * BLOCK), j * LANES:(j + 1) * LANES]
            return lax.dot_general(qblk, kblk, (((1,), (1,)), ((), ())), preferred_element_type=F32)

        def projections(j):
            if has_prev:
                mo_parts.append(jnp.dot(o_prev, wo_ref[:, j * cw:(j + 1) * cw], preferred_element_type=F32))
            if has_next:
                q_piece(h_next, j, rn, qnext)

        def values(j, sc):
            ps, sink_p = [], []
            for i in range(Q_PER_KV):
                sink = sinks_ref[Q_PER_KV * j + i] * LOG2E
                sh = jnp.where(valid, sc[i * BLOCK:(i + 1) * BLOCK], NEG_INF)
                m = jnp.maximum(jnp.max(sh, axis=-1, keepdims=True), sink)
                ps.append(jnp.exp2(sh - m).astype(BF16))
                sink_p.append(jnp.exp2(sink - m))
            vblk = vbuf[pl.ds(r0, 2 * BLOCK), 2 * j * LANES:(2 * j + 2) * LANES]
            ov = jnp.dot(jnp.concatenate(ps, axis=0), vblk, preferred_element_type=F32)
            heads = []
            for i in range(Q_PER_KV):
                num = ov[i * BLOCK:(i + 1) * BLOCK, :LANES]
                den = ov[i * BLOCK:(i + 1) * BLOCK, LANES:] + sink_p[i]
                heads.append(num / den)
            for pp in range(Q_PER_KV // 2):
                col = (Q_PER_KV // 2 * j + pp) * LANES
                o_scr[pl.ds(r0, BLOCK), col:col + LANES] = jnp.where(out_lo, heads[2 * pp], heads[2 * pp + 1]).astype(BF16)

        sc = scores(0)
        for j in range(N_KV_HEADS):
            sc_next = scores(j + 1) if j + 1 < N_KV_HEADS else None
            projections(j)
            values(j, sc)
            sc = sc_next
        if has_prev:
            post_norm(mo_parts, rp)

    h0 = pre_norm(0)
    for j in range(N_KV_HEADS):
        q_piece(h0, j, 0, qa)

    block_step(0, qa, qb, has_prev=False)

    def pair_body(i, carry):
        block_step(2 * i + 1, qb, qa)
        block_step(2 * i + 2, qa, qb)
        return carry

    lax.fori_loop(0, nblk // 2 - 1, pair_body, 0)
    block_step(nblk - 1, qb, qa, has_next=False)
    r_last = (nblk - 1) * BLOCK
    mo_last = jnp.dot(o_scr[r_last:r_last + BLOCK, :], wo_ref[...], preferred_element_type=F32)
    o_ref[0, r_last:r_last + BLOCK, :] = x_ref[0, r_last:r_last + BLOCK, :] + _rms(mo_last, gpost_ref[...])


def _attn_layer(x, gpre, gpost, cos, sin, k, v, wq, wo, sinks, *, layer, tm):
    B, S, D = x.shape
    W = N_KV_HEADS * LANES
    nbt = tm // BLOCK
    xspec = pl.BlockSpec((1, tm, D), lambda b, s: (b, s, 0))
    tspec = pl.BlockSpec((1, tm, LANES), lambda b, s: (b, s, 0))
    cur = pl.BlockSpec((1, tm, W), lambda b, s: (b, s, 0))
    prev = pl.BlockSpec((1, BLOCK, W), lambda b, s: (b, jnp.maximum(s * nbt - 1, 0), 0))
    return pl.pallas_call(
        functools.partial(_attn_kernel, tm=tm),
        grid=(B, S // tm),
        in_specs=[pl.BlockSpec(memory_space=pltpu.SMEM), xspec, _const_spec((1, D)), _const_spec((1, D)),
                  tspec, tspec, cur, prev, cur, prev, _layer_spec(wq.shape, layer), _layer_spec(wo.shape, layer)],
        out_specs=xspec,
        out_shape=jax.ShapeDtypeStruct(x.shape, x.dtype),
        scratch_shapes=[pltpu.VMEM((N_Q_HEADS, BLOCK, LANES), BF16), pltpu.VMEM((N_Q_HEADS, BLOCK, LANES), BF16),
                        pltpu.VMEM((tm + BLOCK, W), BF16), pltpu.VMEM((tm + BLOCK, 2 * W), BF16),
                        pltpu.VMEM((tm, D), BF16)],
        compiler_params=_params(),
        name="attn_layer",
    )(sinks, x, gpre, gpost, cos, sin, k, k, v, v, wq, wo)


def _pair_layout_q(wq):
    lead = wq.shape[:-1]
    w = wq.reshape(*lead, N_Q_HEADS // 2, 2, 2, HALF)
    w = jnp.swapaxes(w, -3, -2)
    return w.reshape(*lead, N_Q_HEADS * HEAD_DIM)


def _dup_layout_k(wk):
    D = wk.shape[0]
    w = wk.reshape(D, N_KV_HEADS, 2, 1, HALF)
    w = jnp.broadcast_to(w, (D, N_KV_HEADS, 2, 2, HALF))
    return w.reshape(D, N_KV_HEADS * LANES)


def _dup_layout_v(wv):
    D = wv.shape[0]
    w = wv.reshape(D, N_KV_HEADS, 1, HEAD_DIM)
    w = jnp.broadcast_to(w, (D, N_KV_HEADS, 2, HEAD_DIM))
    return w.reshape(D, N_KV_HEADS * LANES)


def kernel(x, positions, mix_pre_g, mix_post_g, pool_w, pool_scale, kv_norm_g, w_kv, w_q, w_o, sinks,
           ffn_pre_g, ffn_post_g, ffn_w_in, ffn_conv_w, ffn_conv_b, ffn_w_out):
    B, S, D = x.shape
    tm = min(TILE_ROWS, S)
    depth = mix_pre_g.shape[0]
    n_a = pool_w.shape[0]
    F = ffn_w_out.shape[1]
    hkvd = N_KV_HEADS * HEAD_DIM
    assert S % tm == 0 and tm % (2 * BLOCK) == 0 and F % FFN_CHUNK == 0

    assert n_a >= 1 and depth % n_a == 0
    per = depth // n_a
    win2d, wout2d = ffn_w_in.reshape(depth * D, 2 * F), ffn_w_out.reshape(depth * F, D)
    win_bf, wout_bf = [], []
    wq_all, wo_all = _pair_layout_q(w_q).astype(BF16), w_o.astype(BF16)
    cos = sin = k = v = None
    for layer in range(depth):
        gpre, gpost = mix_pre_g[layer][None, :], mix_post_g[layer][None, :]
        if layer < n_a:
            x, wi, wo_ = _pool_layer(x, gpre, gpost, pool_w[layer].astype(BF16), pool_scale[layer][None, :],
                                     win2d, wout2d, layer, n_a, tm=tm)
            win_bf.append(wi.reshape(per, D, 2 * F))
            wout_bf.append(wo_.reshape(per, F, D))
        else:
            if layer == n_a:
                k, v, cos, sin = _kv_proj(x, kv_norm_g[None, :], positions,
                                          _dup_layout_k(w_kv[:, :hkvd]).astype(BF16),
                                          _dup_layout_v(w_kv[:, hkvd:]).astype(BF16), tm=tm)
            j = layer - n_a
            x = _attn_layer(x, gpre, gpost, cos, sin, k, v, wq_all, wo_all, sinks[j], layer=j, tm=tm)
        x = _ffn_layer(x, ffn_pre_g[layer][None, :], ffn_post_g[layer][None, :],
                       win_bf[layer // per], ffn_conv_w[layer], ffn_conv_b[layer][None, :], wout_bf[layer // per],
                       layer=layer % per, tm=tm)
    return x
```

```python
import functools
import math

import jax
import jax.numpy as jnp
from jax import lax
from jax.experimental import pallas as pl
from jax.experimental.pallas import tpu as pltpu

F32 = jnp.float32
BF16 = jnp.bfloat16

RMS_EPS = 1e-6
POOL_WINDOWS = (2, 4, 8, 16)
HEAD_DIM = 64
HALF = HEAD_DIM // 2
N_Q_HEADS = 16
N_KV_HEADS = 4
Q_PER_KV = N_Q_HEADS // N_KV_HEADS
WINDOW = 128
BLOCK = 128
ROPE_THETA = 10000.0
ATTN_SCALE = 1.0 / math.sqrt(HEAD_DIM)
NEG_INF = -1e30
CONV_WIDTH = 3
LANES = 128
SUBLANES = 8
STAGE_PAD = 8
FFN_CHUNK = 256
VMEM_LIMIT = 56 * 1024 * 1024

TILE_ROWS = 1024
POOL_INPUT_BUFFERS = 3

GELU_C = math.sqrt(2.0 / math.pi)
LOG2E = math.log2(math.e)


def _rms(x, g):
    ms = jnp.mean(x * x, axis=-1, keepdims=True)
    return x * lax.rsqrt(ms + RMS_EPS) * g


def _const_spec(shape):
    nd = len(shape)
    return pl.BlockSpec(shape, lambda b, s: (0,) * nd, pipeline_mode=pl.Buffered(1))


def _layer_spec(shape, layer):
    nd = len(shape) - 1
    return pl.BlockSpec((None,) + tuple(shape[1:]), lambda b, s: (layer,) + (0,) * nd, pipeline_mode=pl.Buffered(1))


def _params():
    return pltpu.CompilerParams(dimension_semantics=("arbitrary", "arbitrary"),
                                vmem_limit_bytes=VMEM_LIMIT)


def _pool_kernel(x_ref, gpre_ref, gpost_ref, wp_ref, scale_ref, win_ref, wout_ref, o_ref, win_bf_ref, wout_bf_ref,
                 stage, hp, *hprev, tm, gc):
    win_bf_ref[...] = win_ref[...].astype(BF16)
    wout_bf_ref[...] = wout_ref[...].astype(BF16)
    s_idx = pl.program_id(1)
    D = hp.shape[1]
    Q = tm // SUBLANES
    P = Q + STAGE_PAD
    nslab = D // LANES

    @pl.when(s_idx == 0)
    def _():
        for prev in hprev:
            prev[...] = jnp.zeros(prev.shape, F32)

    x = x_ref[0]
    h = _rms(x, gpre_ref[...])
    for j in range(nslab):
        for i in range(SUBLANES):
            stage[j, i * P:i * P + Q, :] = h[i * Q:(i + 1) * Q, j * LANES:(j + 1) * LANES]
    for k in range(Q):
        hp[k * SUBLANES:(k + 1) * SUBLANES, :] = jnp.concatenate(
            [stage[j, pl.ds(k, SUBLANES, stride=P), :] for j in range(nslab)], axis=1)

    def shifted(t, sft, prev):
        rows, cols = sft * SUBLANES, t.shape[1]
        last = t[tm - rows:, :]
        sub = lax.broadcasted_iota(jnp.int32, (rows, cols), 0) % SUBLANES
        wrap = jnp.where(sub == SUBLANES - 1, prev[...], last)
        wrap = jnp.concatenate([pltpu.roll(wrap[m * SUBLANES:(m + 1) * SUBLANES, :], 1, 0) for m in range(sft)], axis=0)
        prev[...] = last
        return jnp.concatenate([wrap, t[:tm - rows, :]], axis=0)

    level, sums = hp[...], []
    for gi, (w, prev) in enumerate(zip(POOL_WINDOWS, hprev)):
        assert w == 2 ** (gi + 1)
        level = level + shifted(level, w // 2, prev)
        sums.append(level[:, :gc])
        if level.shape[1] > gc:
            level = level[:, gc:]

    first = jnp.where(s_idx == 0, 1.0, 0.0)
    sub8 = lax.broadcasted_iota(jnp.int32, (SUBLANES, gc), 0)
    ys = []
    for gi, w in enumerate(POOL_WINDOWS):
        own = hp[:, gi * gc:(gi + 1) * gc]
        mean = sums[gi] * (1.0 / w)
        fix = [jnp.where(sub8 == 0, 1.0 + first * (w / (k + 1.0) - 1.0), 1.0) for k in range(w - 1)]
        head = jnp.concatenate(fix, axis=0) * mean[0:(w - 1) * SUBLANES, :]
        mean = jnp.concatenate([head, mean[(w - 1) * SUBLANES:, :]], axis=0)
        d = mean - own
        ys.append(jnp.dot(d.astype(BF16), wp_ref[gi], preferred_element_type=F32))
    m = jnp.concatenate(ys, axis=-1) * scale_ref[...]
    r = _rms(m, gpost_ref[...])
    for k in range(Q):
        for j in range(nslab):
            stage[j, pl.ds(k, SUBLANES, stride=P), :] = r[k * SUBLANES:(k + 1) * SUBLANES, j * LANES:(j + 1) * LANES]
    rn = jnp.concatenate([jnp.concatenate([stage[j, i * P:i * P + Q, :] for i in range(SUBLANES)], axis=0)
                          for j in range(nslab)], axis=1)
    o_ref[0] = x + rn


def _pool_layer(x, gpre, gpost, wp, scale, win, wout, part, nparts, *, tm):
    B, S, D = x.shape
    gc = D // len(POOL_WINDOWS)
    wmax = max(POOL_WINDOWS)
    assert tm // SUBLANES >= wmax
    ns = S // tm
    nsteps = B * ns
    xspec = pl.BlockSpec((1, tm, D), lambda b, s: (b, s, 0))

    def cast_specs(w):
        rows = w.shape[0] // nparts
        blk = rows // nsteps
        assert w.shape[0] % nparts == 0 and rows % nsteps == 0 and blk % (2 * SUBLANES) == 0
        return (pl.BlockSpec((blk, w.shape[1]), lambda b, s: (part * nsteps + b * ns + s, 0)),
                pl.BlockSpec((blk, w.shape[1]), lambda b, s: (b * ns + s, 0)),
                jax.ShapeDtypeStruct((rows, w.shape[1]), BF16))

    win_in, win_out, win_shape = cast_specs(win)
    wout_in, wout_out, wout_shape = cast_specs(wout)
    deep = pl.BlockSpec((1, tm, D), lambda b, s: (b, s, 0), pipeline_mode=pl.Buffered(POOL_INPUT_BUFFERS))

    def outer(x_hbm, gpre_ref, gpost_ref, wp_ref, scale_ref, win_hbm, wout_hbm, o_hbm, win_bf_hbm, wout_bf_hbm,
              stage, hp, *hprev):
        def inner(x_ref, win_ref, wout_ref, o_ref, win_bf_ref, wout_bf_ref):
            _pool_kernel(x_ref, gpre_ref, gpost_ref, wp_ref, scale_ref, win_ref, wout_ref,
                         o_ref, win_bf_ref, wout_bf_ref, stage, hp, *hprev, tm=tm, gc=gc)

        pltpu.emit_pipeline(inner, grid=(B, ns), in_specs=[deep, win_in, wout_in],
                            out_specs=[xspec, win_out, wout_out])(
            x_hbm, win_hbm, wout_hbm, o_hbm, win_bf_hbm, wout_bf_hbm)

    hbm = pl.BlockSpec(memory_space=pl.ANY)
    vmem = pl.BlockSpec(memory_space=pltpu.VMEM)
    return pl.pallas_call(
        outer,
        in_specs=[hbm, vmem, vmem, vmem, vmem, hbm, hbm],
        out_specs=[hbm, hbm, hbm],
        out_shape=[jax.ShapeDtypeStruct(x.shape, x.dtype), win_shape, wout_shape],
        scratch_shapes=[pltpu.VMEM((D // LANES, tm + SUBLANES * STAGE_PAD, LANES), F32),
                        pltpu.VMEM((tm, D), F32)]
                       + [pltpu.VMEM((w // 2 * SUBLANES, D - gi * gc), F32) for gi, w in enumerate(POOL_WINDOWS)],
        compiler_params=pltpu.CompilerParams(vmem_limit_bytes=VMEM_LIMIT),
        name="pool_layer",
    )(x, gpre, gpost, wp, scale, win, wout)


def _gelu_gate(gate, val):
    inner = gate * (GELU_C + (GELU_C * 0.044715) * (gate * gate))
    return (gate * val) * (0.5 * jnp.tanh(inner) + 0.5)


def _ffn_kernel(x_ref, gpre_ref, gpost_ref, win_ref, cw_ref, cb_ref, wout_ref, o_ref,
                stage_in, stage_out, h_scr, act_scr, halo, *, tm, fc, nc):
    F = nc * fc
    D = wout_ref.shape[1]
    Q = tm // SUBLANES
    P = Q + STAGE_PAD
    nslab = D // LANES

    @pl.when(pl.program_id(1) == 0)
    def _():
        halo[...] = jnp.zeros(halo.shape, F32)

    x = x_ref[0]
    h = _rms(x, gpre_ref[...])
    for j in range(nslab):
        for i in range(SUBLANES):
            stage_in[j, i * P:i * P + Q, :] = h[i * Q:(i + 1) * Q, j * LANES:(j + 1) * LANES]
    for k in range(0, Q, 2):
        blk = [jnp.concatenate([stage_in[j, pl.ds(k + d, SUBLANES, stride=P), :] for j in range(nslab)], axis=1)
               for d in range(2)]
        h_scr[k * SUBLANES:(k + 2) * SUBLANES, :] = jnp.concatenate(blk, axis=0).astype(BF16)

    sub = lax.broadcasted_iota(jnp.int32, (SUBLANES, fc), 0)

    def conv(col):
        u = jnp.dot(h_scr[...], win_ref[:, col:col + fc], preferred_element_type=F32)
        last1, last2 = u[tm - SUBLANES:, :], u[tm - 2 * SUBLANES:tm - SUBLANES, :]
        b1 = pltpu.roll(jnp.where(sub == SUBLANES - 1, halo[0:SUBLANES, col:col + fc], last1), 1, 0)
        b2 = pltpu.roll(jnp.where(sub == SUBLANES - 1, halo[SUBLANES:2 * SUBLANES, col:col + fc], last2), 1, 0)
        halo[0:SUBLANES, col:col + fc] = last1
        halo[SUBLANES:2 * SUBLANES, col:col + fc] = last2
        um1 = jnp.concatenate([b1, u[:tm - SUBLANES, :]], axis=0)
        um2 = jnp.concatenate([b2, b1, u[:tm - 2 * SUBLANES, :]], axis=0)
        return (cw_ref[0:1, col:col + fc] * um2 + cw_ref[1:2, col:col + fc] * um1
                + cw_ref[2:3, col:col + fc] * u + cb_ref[:, col:col + fc])

    for c in range(nc):
        gate = conv(c * fc)
        val = conv(F + c * fc)
        act_scr[:, c * fc:(c + 1) * fc] = _gelu_gate(gate, val).astype(BF16)

    f = jnp.dot(act_scr[...], wout_ref[...], preferred_element_type=F32)
    for k in range(Q):
        for j in range(nslab):
            stage_out[j, pl.ds(k, SUBLANES, stride=P), :] = f[k * SUBLANES:(k + 1) * SUBLANES, j * LANES:(j + 1) * LANES]
    fn = jnp.concatenate([jnp.concatenate([stage_out[j, i * P:i * P + Q, :] for i in range(SUBLANES)], axis=0)
                          for j in range(nslab)], axis=1)
    o_ref[0] = x + _rms(fn, gpost_ref[...])


def _ffn_layer(x, gpre, gpost, win, cw, cb, wout, *, layer, tm):
    B, S, D = x.shape
    F = wout.shape[1]
    fc = FFN_CHUNK
    nc = F // fc
    assert cw.shape == (CONV_WIDTH, 2 * F) and tm % (2 * SUBLANES) == 0
    xspec = pl.BlockSpec((1, tm, D), lambda b, s: (b, s, 0))
    stage = pltpu.VMEM((D // LANES, tm + SUBLANES * STAGE_PAD, LANES), F32)
    return pl.pallas_call(
        functools.partial(_ffn_kernel, tm=tm, fc=fc, nc=nc),
        grid=(B, S // tm),
        in_specs=[xspec, _const_spec((1, D)), _const_spec((1, D)), _layer_spec(win.shape, layer),
                  _const_spec(cw.shape), _const_spec(cb.shape), _layer_spec(wout.shape, layer)],
        out_specs=xspec,
        out_shape=jax.ShapeDtypeStruct(x.shape, x.dtype),
        scratch_shapes=[stage, stage,
                        pltpu.VMEM((tm, D), BF16), pltpu.VMEM((tm, F), BF16),
                        pltpu.VMEM(((CONV_WIDTH - 1) * SUBLANES, 2 * F), F32)],
        compiler_params=_params(),
        name="ffn_layer",
    )(x, gpre, gpost, win, cw, cb, wout)


def _rope_tables(pos, invf, sign, tm):
    ngrp = LANES // HALF
    q4 = tm // ngrp
    grp = lax.broadcasted_iota(jnp.int32, (q4, LANES), 1) // HALF
    packed = jnp.zeros((q4, LANES), F32)
    for g in range(ngrp):
        packed = jnp.where(grp == g, pos[g * q4:(g + 1) * q4, :], packed)
    ang = packed * invf
    tables = []
    for table in (jnp.cos(ang), jnp.sin(ang)):
        rolled = [table] + [pltpu.roll(table, HALF * d, 1) for d in range(1, ngrp)]
        quarters = []
        for g in range(ngrp):
            out = rolled[(0 - g) % ngrp]
            for h in range(1, ngrp):
                out = jnp.where(grp == h, rolled[(h - g) % ngrp], out)
            quarters.append(out)
        tables.append(jnp.concatenate(quarters, axis=0))
    return tables[0], tables[1] * sign


def _rope(t, cos, sin):
    return t * cos + pltpu.roll(t, LANES // 2, 1) * sin


def _kv_kernel(x_ref, g_ref, pos_ref, invf_ref, sign_ref, wk_ref, wv_ref, k_ref, v_ref, cos_ref, sin_ref, *, tm):
    cos, sin = _rope_tables(pos_ref[0].astype(F32), invf_ref[...], sign_ref[...], tm)
    cos_ref[0] = cos
    sin_ref[0] = sin
    h = _rms(x_ref[0], g_ref[...]).astype(BF16)
    kd = jnp.dot(h, wk_ref[...], preferred_element_type=F32)
    for j in range(N_KV_HEADS):
        k_ref[0, :, j * LANES:(j + 1) * LANES] = _rope(kd[:, j * LANES:(j + 1) * LANES], cos, sin).astype(BF16)
    v_ref[0] = jnp.dot(h, wv_ref[...], preferred_element_type=F32).astype(BF16)


def _kv_proj(x, g, positions, wk, wv, *, tm):
    B, S, D = x.shape
    W = N_KV_HEADS * LANES
    inv_freq = 1.0 / (ROPE_THETA ** (jnp.arange(0, HEAD_DIM, 2, dtype=F32) / HEAD_DIM))
    invf = jnp.tile(inv_freq, LANES // HALF)[None, :]
    sign = jnp.where(jnp.arange(LANES) < LANES // 2, -1.0, 1.0).astype(F32)[None, :]
    tspec = pl.BlockSpec((1, tm, LANES), lambda b, s: (b, s, 0))
    ospec = pl.BlockSpec((1, tm, W), lambda b, s: (b, s, 0))
    return pl.pallas_call(
        functools.partial(_kv_kernel, tm=tm),
        grid=(B, S // tm),
        in_specs=[pl.BlockSpec((1, tm, D), lambda b, s: (b, s, 0)), _const_spec((1, D)),
                  pl.BlockSpec((1, tm, 1), lambda b, s: (b, s, 0)), _const_spec((1, LANES)), _const_spec((1, LANES)),
                  _const_spec(wk.shape), _const_spec(wv.shape)],
        out_specs=[ospec, ospec, tspec, tspec],
        out_shape=[jax.ShapeDtypeStruct((B, S, W), BF16)] * 2 + [jax.ShapeDtypeStruct((B, S, LANES), F32)] * 2,
        compiler_params=_params(),
        name="kv_proj",
    )(x, g, positions[:, :, None], invf, sign, wk, wv)


def _attn_kernel(sinks_ref, x_ref, gpre_ref, gpost_ref, cos_ref, sin_ref, kc_ref, kp_ref, vc_ref, vp_ref,
                 wq_ref, wo_ref, o_ref, qa, qb, kbuf, vbuf, o_scr, *, tm):
    s_idx = pl.program_id(1)
    nblk = tm // BLOCK
    D = wq_ref.shape[0]
    cw = D // N_KV_HEADS
    lane = lax.broadcasted_iota(jnp.int32, (1, LANES), 1)
    first_of_pair = (lane % HEAD_DIM) < HALF

    def pre_norm(r0):
        return _rms(x_ref[0, pl.ds(r0, BLOCK), :], gpre_ref[...]).astype(BF16)

    def q_piece(h, j, r0, qdst):
        q = jnp.dot(h, wq_ref[:, j * cw:(j + 1) * cw], preferred_element_type=F32)
        cos, sin = cos_ref[0, pl.ds(r0, BLOCK), :], sin_ref[0, pl.ds(r0, BLOCK), :]
        for p in range(cw // LANES):
            qr = _rope(q[:, p * LANES:(p + 1) * LANES], cos, sin) * (ATTN_SCALE * LOG2E)
            qdst[Q_PER_KV * j + 2 * p] = jnp.where(first_of_pair, qr, 0.0).astype(BF16)
            qdst[Q_PER_KV * j + 2 * p + 1] = jnp.where(first_of_pair, 0.0, qr).astype(BF16)

    def post_norm(mo_parts, r0):
        mo = jnp.concatenate(mo_parts, axis=-1)
        o_ref[0, pl.ds(r0, BLOCK), :] = x_ref[0, pl.ds(r0, BLOCK), :] + _rms(mo, gpost_ref[...])

    kbuf[0:BLOCK, :] = kp_ref[0]
    kbuf[BLOCK:BLOCK + tm, :] = kc_ref[0]
    for j in range(N_KV_HEADS):
        vbuf[0:BLOCK, 2 * j * LANES:(2 * j + 1) * LANES] = vp_ref[0, :, j * LANES:(j + 1) * LANES]
        vbuf[BLOCK:BLOCK + tm, 2 * j * LANES:(2 * j + 1) * LANES] = vc_ref[0, :, j * LANES:(j + 1) * LANES]
        vbuf[:, (2 * j + 1) * LANES:(2 * j + 2) * LANES] = jnp.ones((tm + BLOCK, LANES), BF16)

    qi = lax.broadcasted_iota(jnp.int32, (BLOCK, 2 * BLOCK), 0)
    kj = lax.broadcasted_iota(jnp.int32, (BLOCK, 2 * BLOCK), 1)
    rel = BLOCK + qi - kj
    band = (rel >= 0) & (rel < WINDOW)
    out_lo = lax.broadcasted_iota(jnp.int32, (BLOCK, LANES), 1) < HEAD_DIM

    def block_step(jb, qcur, qnext, has_prev=True, has_next=True):
        r0 = pl.multiple_of(jb * BLOCK, BLOCK)
        rp = pl.multiple_of((jb - 1) * BLOCK, BLOCK) if has_prev else None
        rn = pl.multiple_of((jb + 1) * BLOCK, BLOCK) if has_next else None
        o_prev = o_scr[pl.ds(rp, BLOCK), :] if has_prev else None
        h_next = pre_norm(rn) if has_next else None
        seq_start = (s_idx * tm + r0 == 0).astype(jnp.int32)
        valid = band & (kj >= BLOCK * seq_start)
        mo_parts = []

        def scores(j):
            qblk = jnp.concatenate([qcur[Q_PER_KV * j + i] for i in range(Q_PER_KV)], axis=0)
            kblk = kbuf[pl.ds(r0, 2 * BLOCK), j * LANES:(j + 1) * LANES]
            return lax.dot_general(qblk, kblk, (((1,), (1,)), ((), ())), preferred_element_type=F32)

        def projections(j):
            if has_prev:
                mo_parts.append(jnp.dot(o_prev, wo_ref[:, j * cw:(j + 1) * cw], preferred_element_type=F32))
            if has_next:
                q_piece(h_next, j, rn, qnext)

        def values(j, sc):
            ps, sink_p = [], []
            for i in range(Q_PER_KV):
                sink = sinks_ref[Q_PER_KV * j + i] * LOG2E
                sh = jnp.where(valid, sc[i * BLOCK:(i + 1) * BLOCK], NEG_INF)
                m = jnp.maximum(jnp.max(sh, axis=-1, keepdims=True), sink)
                ps.append(jnp.exp2(sh - m).astype(BF16))
                sink_p.append(jnp.exp2(sink - m))
            vblk = vbuf[pl.ds(r0, 2 * BLOCK), 2 * j * LANES:(2 * j + 2) * LANES]
            ov = jnp.dot(jnp.concatenate(ps, axis=0), vblk, preferred_element_type=F32)
            heads = []
            for i in range(Q_PER_KV):
                num = ov[i * BLOCK:(i + 1) * BLOCK, :LANES]
                den = ov[i * BLOCK:(i + 1) * BLOCK, LANES:] + sink_p[i]
                heads.append(num / den)
            for pp in range(Q_PER_KV // 2):
                col = (Q_PER_KV // 2 * j + pp) * LANES
                o_scr[pl.ds(r0, BLOCK), col:col + LANES] = jnp.where(out_lo, heads[2 * pp], heads[2 * pp + 1]).astype(BF16)

        sc = scores(0)
        for j in range(N_KV_HEADS):
            sc_next = scores(j + 1) if j + 1 < N_KV_HEADS else None
            projections(j)
            values(j, sc)
            sc = sc_next
        if has_prev:
            post_norm(mo_parts, rp)

    h0 = pre_norm(0)
    for j in range(N_KV_HEADS):
        q_piece(h0, j, 0, qa)

    block_step(0, qa, qb, has_prev=False)

    def pair_body(i, carry):
        block_step(2 * i + 1, qb, qa)
        block_step(2 * i + 2, qa, qb)
        return carry

    lax.fori_loop(0, nblk // 2 - 1, pair_body, 0)
    block_step(nblk - 1, qb, qa, has_next=False)
    r_last = (nblk - 1) * BLOCK
    mo_last = jnp.dot(o_scr[r_last:r_last + BLOCK, :], wo_ref[...], preferred_element_type=F32)
    o_ref[0, r_last:r_last + BLOCK, :] = x_ref[0, r_last:r_last + BLOCK, :] + _rms(mo_last, gpost_ref[...])


def _attn_layer(x, gpre, gpost, cos, sin, k, v, wq, wo, sinks, *, layer, tm):
    B, S, D = x.shape
    W = N_KV_HEADS * LANES
    nbt = tm // BLOCK
    xspec = pl.BlockSpec((1, tm, D), lambda b, s: (b, s, 0))
    tspec = pl.BlockSpec((1, tm, LANES), lambda b, s: (b, s, 0))
    cur = pl.BlockSpec((1, tm, W), lambda b, s: (b, s, 0))
    prev = pl.BlockSpec((1, BLOCK, W), lambda b, s: (b, jnp.maximum(s * nbt - 1, 0), 0))
    return pl.pallas_call(
        functools.partial(_attn_kernel, tm=tm),
        grid=(B, S // tm),
        in_specs=[pl.BlockSpec(memory_space=pltpu.SMEM), xspec, _const_spec((1, D)), _const_spec((1, D)),
                  tspec, tspec, cur, prev, cur, prev, _layer_spec(wq.shape, layer), _layer_spec(wo.shape, layer)],
        out_specs=xspec,
        out_shape=jax.ShapeDtypeStruct(x.shape, x.dtype),
        scratch_shapes=[pltpu.VMEM((N_Q_HEADS, BLOCK, LANES), BF16), pltpu.VMEM((N_Q_HEADS, BLOCK, LANES), BF16),
                        pltpu.VMEM((tm + BLOCK, W), BF16), pltpu.VMEM((tm + BLOCK, 2 * W), BF16),
                        pltpu.VMEM((tm, D), BF16)],
        compiler_params=_params(),
        name="attn_layer",
    )(sinks, x, gpre, gpost, cos, sin, k, k, v, v, wq, wo)


def _pair_layout_q(wq):
    lead = wq.shape[:-1]
    w = wq.reshape(*lead, N_Q_HEADS // 2, 2, 2, HALF)
    w = jnp.swapaxes(w, -3, -2)
    return w.reshape(*lead, N_Q_HEADS * HEAD_DIM)


def _dup_layout_k(wk):
    D = wk.shape[0]
    w = wk.reshape(D, N_KV_HEADS, 2, 1, HALF)
    w = jnp.broadcast_to(w, (D, N_KV_HEADS, 2, 2, HALF))
    return w.reshape(D, N_KV_HEADS * LANES)


def _dup_layout_v(wv):
    D = wv.shape[0]
    w = wv.reshape(D, N_KV_HEADS, 1, HEAD_DIM)
    w = jnp.broadcast_to(w, (D, N_KV_HEADS, 2, HEAD_DIM))
    return w.reshape(D, N_KV_HEADS * LANES)


def kernel(x, positions, mix_pre_g, mix_post_g, pool_w, pool_scale, kv_norm_g, w_kv, w_q, w_o, sinks,
           ffn_pre_g, ffn_post_g, ffn_w_in, ffn_conv_w, ffn_conv_b, ffn_w_out):
    B, S, D = x.shape
    tm = min(TILE_ROWS, S)
    depth = mix_pre_g.shape[0]
    n_a = pool_w.shape[0]
    F = ffn_w_out.shape[1]
    hkvd = N_KV_HEADS * HEAD_DIM
    assert S % tm == 0 and tm % (2 * BLOCK) == 0 and F % FFN_CHUNK == 0

    assert n_a >= 1 and depth % n_a == 0
    per = depth // n_a
    win2d, wout2d = ffn_w_in.reshape(depth * D, 2 * F), ffn_w_out.reshape(depth * F, D)
    win_bf, wout_bf = [], []
    wq_all, wo_all = _pair_layout_q(w_q).astype(BF16), w_o.astype(BF16)
    cos = sin = k = v = None
    for layer in range(depth):
        gpre, gpost = mix_pre_g[layer][None, :], mix_post_g[layer][None, :]
        if layer < n_a:
            x, wi, wo_ = _pool_layer(x, gpre, gpost, pool_w[layer].astype(BF16), pool_scale[layer][None, :],
                                     win2d, wout2d, layer, n_a, tm=tm)
            win_bf.append(wi.reshape(per, D, 2 * F))
            wout_bf.append(wo_.reshape(per, F, D))
        else:
            if layer == n_a:
                k, v, cos, sin = _kv_proj(x, kv_norm_g[None, :], positions,
                                          _dup_layout_k(w_kv[:, :hkvd]).astype(BF16),
                                          _dup_layout_v(w_kv[:, hkvd:]).astype(BF16), tm=tm)
            j = layer - n_a
            x = _attn_layer(x, gpre, gpost, cos, sin, k, v, wq_all, wo_all, sinks[j], layer=j, tm=tm)
        x = _ffn_layer(x, ffn_pre_g[layer][None, :], ffn_post_g[layer][None, :],
                       win_bf[layer // per], ffn_conv_w[layer], ffn_conv_b[layer][None, :], wout_bf[layer // per],
                       layer=layer % per, tm=tm)
    return x
```

```python
import functools
import math

import jax
import jax.numpy as jnp
from jax import lax
from jax.experimental import pallas as pl
from jax.experimental.pallas import tpu as pltpu

F32 = jnp.float32
BF16 = jnp.bfloat16

RMS_EPS = 1e-6
POOL_WINDOWS = (2, 4, 8, 16)
HEAD_DIM = 64
HALF = HEAD_DIM // 2
N_Q_HEADS = 16
N_KV_HEADS = 4
Q_PER_KV = N_Q_HEADS // N_KV_HEADS
WINDOW = 128
BLOCK = 128
ROPE_THETA = 10000.0
ATTN_SCALE = 1.0 / math.sqrt(HEAD_DIM)
NEG_INF = -1e30
CONV_WIDTH = 3
LANES = 128
SUBLANES = 8
STAGE_PAD = 8
FFN_CHUNK = 256
VMEM_LIMIT = 56 * 1024 * 1024

TILE_ROWS = 1024
POOL_INPUT_BUFFERS = 3

GELU_C = math.sqrt(2.0 / math.pi)
LOG2E = math.log2(math.e)


def _rms(x, g):
    ms = jnp.mean(x * x, axis=-1, keepdims=True)
    return x * lax.rsqrt(ms + RMS_EPS) * g


def _const_spec(shape):
    nd = len(shape)
    return pl.BlockSpec(shape, lambda b, s: (0,) * nd, pipeline_mode=pl.Buffered(1))


def _layer_spec(shape, layer):
    nd = len(shape) - 1
    return pl.BlockSpec((None,) + tuple(shape[1:]), lambda b, s: (layer,) + (0,) * nd, pipeline_mode=pl.Buffered(1))


def _params():
    return pltpu.CompilerParams(dimension_semantics=("arbitrary", "arbitrary"),
                                vmem_limit_bytes=VMEM_LIMIT)


def _pool_kernel(x_ref, gpre_ref, gpost_ref, wp_ref, scale_ref, win_ref, wout_ref, o_ref, win_bf_ref, wout_bf_ref,
                 stage, hp, *hprev, tm, gc):
    win_bf_ref[...] = win_ref[...].astype(BF16)
    wout_bf_ref[...] = wout_ref[...].astype(BF16)
    s_idx = pl.program_id(1)
    D = hp.shape[1]
    Q = tm // SUBLANES
    P = Q + STAGE_PAD
    nslab = D // LANES

    @pl.when(s_idx == 0)
    def _():
        for prev in hprev:
            prev[...] = jnp.zeros(prev.shape, F32)

    x = x_ref[0]
    h = _rms(x, gpre_ref[...])
    for j in range(nslab):
        for i in range(SUBLANES):
            stage[j, i * P:i * P + Q, :] = h[i * Q:(i + 1) * Q, j * LANES:(j + 1) * LANES]
    for k in range(Q):
        hp[k * SUBLANES:(k + 1) * SUBLANES, :] = jnp.concatenate(
            [stage[j, pl.ds(k, SUBLANES, stride=P), :] for j in range(nslab)], axis=1)

    def shifted(t, sft, prev):
        rows, cols = sft * SUBLANES, t.shape[1]
        last = t[tm - rows:, :]
        sub = lax.broadcasted_iota(jnp.int32, (rows, cols), 0) % SUBLANES
        wrap = jnp.where(sub == SUBLANES - 1, prev[...], last)
        wrap = jnp.concatenate([pltpu.roll(wrap[m * SUBLANES:(m + 1) * SUBLANES, :], 1, 0) for m in range(sft)], axis=0)
        prev[...] = last
        return jnp.concatenate([wrap, t[:tm - rows, :]], axis=0)

    level, sums = hp[...], []
    for gi, (w, prev) in enumerate(zip(POOL_WINDOWS, hprev)):
        assert w == 2 ** (gi + 1)
        level = level + shifted(level, w // 2, prev)
        sums.append(level[:, :gc])
        if level.shape[1] > gc:
            level = level[:, gc:]

    first = jnp.where(s_idx == 0, 1.0, 0.0)
    sub8 = lax.broadcasted_iota(jnp.int32, (SUBLANES, gc), 0)
    ys = []
    for gi, w in enumerate(POOL_WINDOWS):
        own = hp[:, gi * gc:(gi + 1) * gc]
        mean = sums[gi] * (1.0 / w)
        fix = [jnp.where(sub8 == 0, 1.0 + first * (w / (k + 1.0) - 1.0), 1.0) for k in range(w - 1)]
        head = jnp.concatenate(fix, axis=0) * mean[0:(w - 1) * SUBLANES, :]
        mean = jnp.concatenate([head, mean[(w - 1) * SUBLANES:, :]], axis=0)
        d = mean - own
        ys.append(jnp.dot(d.astype(BF16), wp_ref[gi], preferred_element_type=F32))
    m = jnp.concatenate(ys, axis=-1) * scale_ref[...]
    r = _rms(m, gpost_ref[...])
    for k in range(Q):
        for j in range(nslab):
            stage[j, pl.ds(k, SUBLANES, stride=P), :] = r[k * SUBLANES:(k + 1) * SUBLANES, j * LANES:(j + 1) * LANES]
    rn = jnp.concatenate([jnp.concatenate([stage[j, i * P:i * P + Q, :] for i in range(SUBLANES)], axis=0)
                          for j in range(nslab)], axis=1)
    o_ref[0] = x + rn


def _pool_layer(x, gpre, gpost, wp, scale, win, wout, part, nparts, *, tm):
    B, S, D = x.shape
    gc = D // len(POOL_WINDOWS)
    wmax = max(POOL_WINDOWS)
    assert tm // SUBLANES >= wmax
    ns = S // tm
    nsteps = B * ns
    xspec = pl.BlockSpec((1, tm, D), lambda b, s: (b, s, 0))

    def cast_specs(w):
        rows = w.shape[0] // nparts
        blk = rows // nsteps
        assert w.shape[0] % nparts == 0 and rows % nsteps == 0 and blk % (2 * SUBLANES) == 0
        return (pl.BlockSpec((blk, w.shape[1]), lambda b, s: (part * nsteps + b * ns + s, 0)),
                pl.BlockSpec((blk, w.shape[1]), lambda b, s: (b * ns + s, 0)),
                jax.ShapeDtypeStruct((rows, w.shape[1]), BF16))

    win_in, win_out, win_shape = cast_specs(win)
    wout_in, wout_out, wout_shape = cast_specs(wout)
    deep = pl.BlockSpec((1, tm, D), lambda b, s: (b, s, 0), pipeline_mode=pl.Buffered(POOL_INPUT_BUFFERS))

    def outer(x_hbm, gpre_ref, gpost_ref, wp_ref, scale_ref, win_hbm, wout_hbm, o_hbm, win_bf_hbm, wout_bf_hbm,
              stage, hp, *hprev):
        def inner(x_ref, win_ref, wout_ref, o_ref, win_bf_ref, wout_bf_ref):
            _pool_kernel(x_ref, gpre_ref, gpost_ref, wp_ref, scale_ref, win_ref, wout_ref,
                         o_ref, win_bf_ref, wout_bf_ref, stage, hp, *hprev, tm=tm, gc=gc)

        pltpu.emit_pipeline(inner, grid=(B, ns), in_specs=[deep, win_in, wout_in],
                            out_specs=[xspec, win_out, wout_out])(
            x_hbm, win_hbm, wout_hbm, o_hbm, win_bf_hbm, wout_bf_hbm)

    hbm = pl.BlockSpec(memory_space=pl.ANY)
    vmem = pl.BlockSpec(memory_space=pltpu.VMEM)
    return pl.pallas_call(
        outer,
        in_specs=[hbm, vmem, vmem, vmem, vmem, hbm, hbm],
        out_specs=[hbm, hbm, hbm],
        out_shape=[jax.ShapeDtypeStruct(x.shape, x.dtype), win_shape, wout_shape],
        scratch_shapes=[pltpu.VMEM((D // LANES, tm + SUBLANES * STAGE_PAD, LANES), F32),
                        pltpu.VMEM((tm, D), F32)]
                       + [pltpu.VMEM((w // 2 * SUBLANES, D - gi * gc), F32) for gi, w in enumerate(POOL_WINDOWS)],
        compiler_params=pltpu.CompilerParams(vmem_limit_bytes=VMEM_LIMIT),
        name="pool_layer",
    )(x, gpre, gpost, wp, scale, win, wout)


def _gelu_gate(gate, val):
    inner = gate * (GELU_C + (GELU_C * 0.044715) * (gate * gate))
    return (gate * val) * (0.5 * jnp.tanh(inner) + 0.5)


def _ffn_kernel(x_ref, gpre_ref, gpost_ref, win_ref, cw_ref, cb_ref, wout_ref, o_ref,
                stage_in, stage_out, h_scr, act_scr, halo, *, tm, fc, nc):
    F = nc * fc
    D = wout_ref.shape[1]
    Q = tm // SUBLANES
    P = Q + STAGE_PAD
    nslab = D // LANES

    @pl.when(pl.program_id(1) == 0)
    def _():
        halo[...] = jnp.zeros(halo.shape, F32)

    x = x_ref[0]
    h = _rms(x, gpre_ref[...])
    for j in range(nslab):
        for i in range(SUBLANES):
            stage_in[j, i * P:i * P + Q, :] = h[i * Q:(i + 1) * Q, j * LANES:(j + 1) * LANES]
    for k in range(0, Q, 2):
        blk = [jnp.concatenate([stage_in[j, pl.ds(k + d, SUBLANES, stride=P), :] for j in range(nslab)], axis=1)
               for d in range(2)]
        h_scr[k * SUBLANES:(k + 2) * SUBLANES, :] = jnp.concatenate(blk, axis=0).astype(BF16)

    sub = lax.broadcasted_iota(jnp.int32, (SUBLANES, fc), 0)

    def conv(col):
        u = jnp.dot(h_scr[...], win_ref[:, col:col + fc], preferred_element_type=F32)
        last1, last2 = u[tm - SUBLANES:, :], u[tm - 2 * SUBLANES:tm - SUBLANES, :]
        b1 = pltpu.roll(jnp.where(sub == SUBLANES - 1, halo[0:SUBLANES, col:col + fc], last1), 1, 0)
        b2 = pltpu.roll(jnp.where(sub == SUBLANES - 1, halo[SUBLANES:2 * SUBLANES, col:col + fc], last2), 1, 0)
        halo[0:SUBLANES, col:col + fc] = last1
        halo[SUBLANES:2 * SUBLANES, col:col + fc] = last2
        um1 = jnp.concatenate([b1, u[:tm - SUBLANES, :]], axis=0)
        um2 = jnp.concatenate([b2, b1, u[:tm - 2 * SUBLANES, :]], axis=0)
        return (cw_ref[0:1, col:col + fc] * um2 + cw_ref[1:2, col:col + fc] * um1
                + cw_ref[2:3, col:col + fc] * u + cb_ref[:, col:col + fc])

    for c in range(nc):
        gate = conv(c * fc)
        val = conv(F + c * fc)
        act_scr[:, c * fc:(c + 1) * fc] = _gelu_gate(gate, val).astype(BF16)

    f = jnp.dot(act_scr[...], wout_ref[...], preferred_element_type=F32)
    for k in range(Q):
        for j in range(nslab):
            stage_out[j, pl.ds(k, SUBLANES, stride=P), :] = f[k * SUBLANES:(k + 1) * SUBLANES, j * LANES:(j + 1) * LANES]
    fn = jnp.concatenate([jnp.concatenate([stage_out[j, i * P:i * P + Q, :] for i in range(SUBLANES)], axis=0)
                          for j in range(nslab)], axis=1)
    o_ref[0] = x + _rms(fn, gpost_ref[...])


def _ffn_layer(x, gpre, gpost, win, cw, cb, wout, *, layer, tm):
    B, S, D = x.shape
    F = wout.shape[1]
    fc = FFN_CHUNK
    nc = F // fc
    assert cw.shape == (CONV_WIDTH, 2 * F) and tm % (2 * SUBLANES) == 0
    xspec = pl.BlockSpec((1, tm, D), lambda b, s: (b, s, 0))
    stage = pltpu.VMEM((D // LANES, tm + SUBLANES * STAGE_PAD, LANES), F32)

    def outer(x_hbm, gpre_ref, gpost_ref, win_hbm, cw_ref, cb_ref, wout_hbm, o_hbm,
              win_ref, wout_ref, stage_in, stage_out, h_scr, act_scr, halo):
        pltpu.sync_copy(win_hbm.at[layer], win_ref)
        pltpu.sync_copy(wout_hbm.at[layer], wout_ref)

        def inner(x_ref, o_ref):
            _ffn_kernel(x_ref, gpre_ref, gpost_ref, win_ref, cw_ref, cb_ref, wout_ref, o_ref,
                        stage_in, stage_out, h_scr, act_scr, halo, tm=tm, fc=fc, nc=nc)

        pltpu.emit_pipeline(inner, grid=(B, S // tm), in_specs=[xspec], out_specs=[xspec])(x_hbm, o_hbm)

    hbm = pl.BlockSpec(memory_space=pl.ANY)
    vmem = pl.BlockSpec(memory_space=pltpu.VMEM)
    return pl.pallas_call(
        outer,
        in_specs=[hbm, vmem, vmem, hbm, vmem, vmem, hbm],
        out_specs=hbm,
        out_shape=jax.ShapeDtypeStruct(x.shape, x.dtype),
        scratch_shapes=[pltpu.VMEM(win.shape[1:], BF16), pltpu.VMEM(wout.shape[1:], BF16), stage, stage,
                        pltpu.VMEM((tm, D), BF16), pltpu.VMEM((tm, F), BF16),
                        pltpu.VMEM(((CONV_WIDTH - 1) * SUBLANES, 2 * F), F32)],
        compiler_params=pltpu.CompilerParams(vmem_limit_bytes=VMEM_LIMIT),
        name="ffn_layer",
    )(x, gpre, gpost, win, cw, cb, wout)


def _rope_tables(pos, invf, sign, tm):
    ngrp = LANES // HALF
    q4 = tm // ngrp
    grp = lax.broadcasted_iota(jnp.int32, (q4, LANES), 1) // HALF
    packed = jnp.zeros((q4, LANES), F32)
    for g in range(ngrp):
        packed = jnp.where(grp == g, pos[g * q4:(g + 1) * q4, :], packed)
    ang = packed * invf
    tables = []
    for table in (jnp.cos(ang), jnp.sin(ang)):
        rolled = [table] + [pltpu.roll(table, HALF * d, 1) for d in range(1, ngrp)]
        quarters = []
        for g in range(ngrp):
            out = rolled[(0 - g) % ngrp]
            for h in range(1, ngrp):
                out = jnp.where(grp == h, rolled[(h - g) % ngrp], out)
            quarters.append(out)
        tables.append(jnp.concatenate(quarters, axis=0))
    return tables[0], tables[1] * sign


def _rope(t, cos, sin):
    return t * cos + pltpu.roll(t, LANES // 2, 1) * sin


def _kv_kernel(x_ref, g_ref, pos_ref, invf_ref, sign_ref, wk_ref, wv_ref, k_ref, v_ref, cos_ref, sin_ref, *, tm):
    cos, sin = _rope_tables(pos_ref[0].astype(F32), invf_ref[...], sign_ref[...], tm)
    cos_ref[0] = cos
    sin_ref[0] = sin
    h = _rms(x_ref[0], g_ref[...]).astype(BF16)
    kd = jnp.dot(h, wk_ref[...], preferred_element_type=F32)
    for j in range(N_KV_HEADS):
        k_ref[0, :, j * LANES:(j + 1) * LANES] = _rope(kd[:, j * LANES:(j + 1) * LANES], cos, sin).astype(BF16)
    v_ref[0] = jnp.dot(h, wv_ref[...], preferred_element_type=F32).astype(BF16)


def _kv_proj(x, g, positions, wk, wv, *, tm):
    B, S, D = x.shape
    W = N_KV_HEADS * LANES
    inv_freq = 1.0 / (ROPE_THETA ** (jnp.arange(0, HEAD_DIM, 2, dtype=F32) / HEAD_DIM))
    invf = jnp.tile(inv_freq, LANES // HALF)[None, :]
    sign = jnp.where(jnp.arange(LANES) < LANES // 2, -1.0, 1.0).astype(F32)[None, :]
    tspec = pl.BlockSpec((1, tm, LANES), lambda b, s: (b, s, 0))
    ospec = pl.BlockSpec((1, tm, W), lambda b, s: (b, s, 0))
    return pl.pallas_call(
        functools.partial(_kv_kernel, tm=tm),
        grid=(B, S // tm),
        in_specs=[pl.BlockSpec((1, tm, D), lambda b, s: (b, s, 0)), _const_spec((1, D)),
                  pl.BlockSpec((1, tm, 1), lambda b, s: (b, s, 0)), _const_spec((1, LANES)), _const_spec((1, LANES)),
                  _const_spec(wk.shape), _const_spec(wv.shape)],
        out_specs=[ospec, ospec, tspec, tspec],
        out_shape=[jax.ShapeDtypeStruct((B, S, W), BF16)] * 2 + [jax.ShapeDtypeStruct((B, S, LANES), F32)] * 2,
        compiler_params=_params(),
        name="kv_proj",
    )(x, g, positions[:, :, None], invf, sign, wk, wv)


def _attn_kernel(sinks_ref, x_ref, gpre_ref, gpost_ref, cos_ref, sin_ref, kc_ref, kp_ref, vc_ref, vp_ref,
                 wq_ref, wo_ref, o_ref, qa, qb, kbuf, vbuf, o_scr, *, tm):
    s_idx = pl.program_id(1)
    nblk = tm // BLOCK
    D = wq_ref.shape[0]
    cw = D // N_KV_HEADS
    lane = lax.broadcasted_iota(jnp.int32, (1, LANES), 1)
    first_of_pair = (lane % HEAD_DIM) < HALF

    def pre_norm(r0):
        return _rms(x_ref[0, pl.ds(r0, BLOCK), :], gpre_ref[...]).astype(BF16)

    def q_piece(h, j, r0, qdst):
        q = jnp.dot(h, wq_ref[:, j * cw:(j + 1) * cw], preferred_element_type=F32)
        cos, sin = cos_ref[0, pl.ds(r0, BLOCK), :], sin_ref[0, pl.ds(r0, BLOCK), :]
        for p in range(cw // LANES):
            qr = _rope(q[:, p * LANES:(p + 1) * LANES], cos, sin) * (ATTN_SCALE * LOG2E)
            qdst[Q_PER_KV * j + 2 * p] = jnp.where(first_of_pair, qr, 0.0).astype(BF16)
            qdst[Q_PER_KV * j + 2 * p + 1] = jnp.where(first_of_pair, 0.0, qr).astype(BF16)

    def post_norm(mo_parts, r0):
        mo = jnp.concatenate(mo_parts, axis=-1)
        o_ref[0, pl.ds(r0, BLOCK), :] = x_ref[0, pl.ds(r0, BLOCK), :] + _rms(mo, gpost_ref[...])

    kbuf[0:BLOCK, :] = kp_ref[0]
    kbuf[BLOCK:BLOCK + tm, :] = kc_ref[0]
    for j in range(N_KV_HEADS):
        vbuf[0:BLOCK, 2 * j * LANES:(2 * j + 1) * LANES] = vp_ref[0, :, j * LANES:(j + 1) * LANES]
        vbuf[BLOCK:BLOCK + tm, 2 * j * LANES:(2 * j + 1) * LANES] = vc_ref[0, :, j * LANES:(j + 1) * LANES]
        vbuf[:, (2 * j + 1) * LANES:(2 * j + 2) * LANES] = jnp.ones((tm + BLOCK, LANES), BF16)

    qi = lax.broadcasted_iota(jnp.int32, (BLOCK, 2 * BLOCK), 0)
    kj = lax.broadcasted_iota(jnp.int32, (BLOCK, 2 * BLOCK), 1)
    rel = BLOCK + qi - kj
    band = (rel >= 0) & (rel < WINDOW)
    out_lo = lax.broadcasted_iota(jnp.int32, (BLOCK, LANES), 1) < HEAD_DIM

    def block_step(jb, qcur, qnext, has_prev=True, has_next=True):
        r0 = pl.multiple_of(jb * BLOCK, BLOCK)
        rp = pl.multiple_of((jb - 1) * BLOCK, BLOCK) if has_prev else None
        rn = pl.multiple_of((jb + 1) * BLOCK, BLOCK) if has_next else None
        o_prev = o_scr[pl.ds(rp, BLOCK), :] if has_prev else None
        h_next = pre_norm(rn) if has_next else None
        seq_start = (s_idx * tm + r0 == 0).astype(jnp.int32)
        valid = band & (kj >= BLOCK * seq_start)
        mo_parts = []

        def scores(j):
            qblk = jnp.concatenate([qcur[Q_PER_KV * j + i] for i in range(Q_PER_KV)], axis=0)
            kblk = kbuf[pl.ds(r0, 2 * BLOCK), j * LANES:(j + 1) * LANES]
            return lax.dot_general(qblk, kblk, (((1,), (1,)), ((), ())), preferred_element_type=F32)

        def projections(j):
            if has_prev:
                mo_parts.append(jnp.dot(o_prev, wo_ref[:, j * cw:(j + 1) * cw], preferred_element_type=F32))
            if has_next:
                q_piece(h_next, j, rn, qnext)

        def values(j, sc):
            ps, sink_p = [], []
            for i in range(Q_PER_KV):
                sink = sinks_ref[Q_PER_KV * j + i] * LOG2E
                sh = jnp.where(valid, sc[i * BLOCK:(i + 1) * BLOCK], NEG_INF)
                m = jnp.maximum(jnp.max(sh, axis=-1, keepdims=True), sink)
                ps.append(jnp.exp2(sh - m).astype(BF16))
                sink_p.append(jnp.exp2(sink - m))
            vblk = vbuf[pl.ds(r0, 2 * BLOCK), 2 * j * LANES:(2 * j + 2) * LANES]
            ov = jnp.dot(jnp.concatenate(ps, axis=0), vblk, preferred_element_type=F32)
            heads = []
            for i in range(Q_PER_KV):
                num = ov[i * BLOCK:(i + 1) * BLOCK, :LANES]
                den = ov[i * BLOCK:(i + 1) * BLOCK, LANES:] + sink_p[i]
                heads.append(num / den)
            for pp in range(Q_PER_KV // 2):
                col = (Q_PER_KV // 2 * j + pp) * LANES
                o_scr[pl.ds(r0, BLOCK), col:col + LANES] = jnp.where(out_lo, heads[2 * pp], heads[2 * pp + 1]).astype(BF16)

        sc = scores(0)
        for j in range(N_KV_HEADS):
            sc_next = scores(j + 1) if j + 1 < N_KV_HEADS else None
            projections(j)
            values(j, sc)
            sc = sc_next
        if has_prev:
            post_norm(mo_parts, rp)

    h0 = pre_norm(0)
    for j in range(N_KV_HEADS):
        q_piece(h0, j, 0, qa)

    block_step(0, qa, qb, has_prev=False)

    def pair_body(i, carry):
        block_step(2 * i + 1, qb, qa)
        block_step(2 * i + 2, qa, qb)
        return carry

    lax.fori_loop(0, nblk // 2 - 1, pair_body, 0)
    block_step(nblk - 1, qb, qa, has_next=False)
    r_last = (nblk - 1) * BLOCK
    mo_last = jnp.dot(o_scr[r_last:r_last + BLOCK, :], wo_ref[...], preferred_element_type=F32)
    o_ref[0, r_last:r_last + BLOCK, :] = x_ref[0, r_last:r_last + BLOCK, :] + _rms(mo_last, gpost_ref[...])


def _attn_layer(x, gpre, gpost, cos, sin, k, v, wq, wo, sinks, *, layer, tm):
    B, S, D = x.shape
    W = N_KV_HEADS * LANES
    nbt = tm // BLOCK
    xspec = pl.BlockSpec((1, tm, D), lambda b, s: (b, s, 0))
    tspec = pl.BlockSpec((1, tm, LANES), lambda b, s: (b, s, 0))
    cur = pl.BlockSpec((1, tm, W), lambda b, s: (b, s, 0))
    prev = pl.BlockSpec((1, BLOCK, W), lambda b, s: (b, jnp.maximum(s * nbt - 1, 0), 0))
    return pl.pallas_call(
        functools.partial(_attn_kernel, tm=tm),
        grid=(B, S // tm),
        in_specs=[pl.BlockSpec(memory_space=pltpu.SMEM), xspec, _const_spec((1, D)), _const_spec((1, D)),
                  tspec, tspec, cur, prev, cur, prev, _layer_spec(wq.shape, layer), _layer_spec(wo.shape, layer)],
        out_specs=xspec,
        out_shape=jax.ShapeDtypeStruct(x.shape, x.dtype),
        scratch_shapes=[pltpu.VMEM((N_Q_HEADS, BLOCK, LANES), BF16), pltpu.VMEM((N_Q_HEADS, BLOCK, LANES), BF16),
                        pltpu.VMEM((tm + BLOCK, W), BF16), pltpu.VMEM((tm + BLOCK, 2 * W), BF16),
                        pltpu.VMEM((tm, D), BF16)],
        compiler_params=_params(),
        name="attn_layer",
    )(sinks, x, gpre, gpost, cos, sin, k, k, v, v, wq, wo)


def _pair_layout_q(wq):
    lead = wq.shape[:-1]
    w = wq.reshape(*lead, N_Q_HEADS // 2, 2, 2, HALF)
    w = jnp.swapaxes(w, -3, -2)
    return w.reshape(*lead, N_Q_HEADS * HEAD_DIM)


def _dup_layout_k(wk):
    D = wk.shape[0]
    w = wk.reshape(D, N_KV_HEADS, 2, 1, HALF)
    w = jnp.broadcast_to(w, (D, N_KV_HEADS, 2, 2, HALF))
    return w.reshape(D, N_KV_HEADS * LANES)


def _dup_layout_v(wv):
    D = wv.shape[0]
    w = wv.reshape(D, N_KV_HEADS, 1, HEAD_DIM)
    w = jnp.broadcast_to(w, (D, N_KV_HEADS, 2, HEAD_DIM))
    return w.reshape(D, N_KV_HEADS * LANES)


def kernel(x, positions, mix_pre_g, mix_post_g, pool_w, pool_scale, kv_norm_g, w_kv, w_q, w_o, sinks,
           ffn_pre_g, ffn_post_g, ffn_w_in, ffn_conv_w, ffn_conv_b, ffn_w_out):
    B, S, D = x.shape
    tm = min(TILE_ROWS, S)
    depth = mix_pre_g.shape[0]
    n_a = pool_w.shape[0]
    F = ffn_w_out.shape[1]
    hkvd = N_KV_HEADS * HEAD_DIM
    assert S % tm == 0 and tm % (2 * BLOCK) == 0 and F % FFN_CHUNK == 0

    assert n_a >= 1 and depth % n_a == 0
    per = depth // n_a
    win2d, wout2d = ffn_w_in.reshape(depth * D, 2 * F), ffn_w_out.reshape(depth * F, D)
    win_bf, wout_bf = [], []
    wq_all, wo_all = _pair_layout_q(w_q).astype(BF16), w_o.astype(BF16)
    cos = sin = k = v = None
    for layer in range(depth):
        gpre, gpost = mix_pre_g[layer][None, :], mix_post_g[layer][None, :]
        if layer < n_a:
            x, wi, wo_ = _pool_layer(x, gpre, gpost, pool_w[layer].astype(BF16), pool_scale[layer][None, :],
                                     win2d, wout2d, layer, n_a, tm=tm)
            win_bf.append(wi.reshape(per, D, 2 * F))
            wout_bf.append(wo_.reshape(per, F, D))
        else:
            if layer == n_a:
                k, v, cos, sin = _kv_proj(x, kv_norm_g[None, :], positions,
                                          _dup_layout_k(w_kv[:, :hkvd]).astype(BF16),
                                          _dup_layout_v(w_kv[:, hkvd:]).astype(BF16), tm=tm)
            j = layer - n_a
            x = _attn_layer(x, gpre, gpost, cos, sin, k, v, wq_all, wo_all, sinks[j], layer=j, tm=tm)
        x = _ffn_layer(x, ffn_pre_g[layer][None, :], ffn_post_g[layer][None, :],
                       win_bf[layer // per], ffn_conv_w[layer], ffn_conv_b[layer][None, :], wout_bf[layer // per],
                       layer=layer % per, tm=tm)
    return x
```

```python
import functools
import math

import jax
import jax.numpy as jnp
from jax import lax
from jax.experimental import pallas as pl
from jax.experimental.pallas import tpu as pltpu

F32 = jnp.float32
BF16 = jnp.bfloat16

RMS_EPS = 1e-6
POOL_WINDOWS = (2, 4, 8, 16)
HEAD_DIM = 64
HALF = HEAD_DIM // 2
N_Q_HEADS = 16
N_KV_HEADS = 4
Q_PER_KV = N_Q_HEADS // N_KV_HEADS
WINDOW = 128
BLOCK = 128
ROPE_THETA = 10000.0
ATTN_SCALE = 1.0 / math.sqrt(HEAD_DIM)
NEG_INF = -1e30
CONV_WIDTH = 3
LANES = 128
SUBLANES = 8
STAGE_PAD = 8
FFN_CHUNK = 256
VMEM_LIMIT = 56 * 1024 * 1024

TILE_ROWS = 1024

GELU_C = math.sqrt(2.0 / math.pi)
LOG2E = math.log2(math.e)


def _rms(x, g):
    ms = jnp.mean(x * x, axis=-1, keepdims=True)
    return x * lax.rsqrt(ms + RMS_EPS) * g


def _const_spec(shape):
    nd = len(shape)
    return pl.BlockSpec(shape, lambda b, s: (0,) * nd, pipeline_mode=pl.Buffered(1))


def _layer_spec(shape, layer):
    nd = len(shape) - 1
    return pl.BlockSpec((None,) + tuple(shape[1:]), lambda b, s: (layer,) + (0,) * nd, pipeline_mode=pl.Buffered(1))


def _params():
    return pltpu.CompilerParams(dimension_semantics=("arbitrary", "arbitrary"),
                                vmem_limit_bytes=VMEM_LIMIT)


def _pool_kernel(x_ref, gpre_ref, gpost_ref, wp_ref, scale_ref, win_ref, wout_ref, o_ref, win_bf_ref, wout_bf_ref,
                 stage, hp, *hprev, tm, gc):
    win_bf_ref[...] = win_ref[...].astype(BF16)
    wout_bf_ref[...] = wout_ref[...].astype(BF16)
    s_idx = pl.program_id(1)
    D = hp.shape[1]
    Q = tm // SUBLANES
    P = Q + STAGE_PAD
    nslab = D // LANES

    @pl.when(s_idx == 0)
    def _():
        for prev in hprev:
            prev[...] = jnp.zeros(prev.shape, F32)

    x = x_ref[0]
    h = _rms(x, gpre_ref[...])
    for j in range(nslab):
        for i in range(SUBLANES):
            stage[j, i * P:i * P + Q, :] = h[i * Q:(i + 1) * Q, j * LANES:(j + 1) * LANES]
    for k in range(Q):
        hp[k * SUBLANES:(k + 1) * SUBLANES, :] = jnp.concatenate(
            [stage[j, pl.ds(k, SUBLANES, stride=P), :] for j in range(nslab)], axis=1)

    def shifted(t, sft, prev):
        rows, cols = sft * SUBLANES, t.shape[1]
        last = t[tm - rows:, :]
        sub = lax.broadcasted_iota(jnp.int32, (rows, cols), 0) % SUBLANES
        wrap = jnp.where(sub == SUBLANES - 1, prev[...], last)
        wrap = jnp.concatenate([pltpu.roll(wrap[m * SUBLANES:(m + 1) * SUBLANES, :], 1, 0) for m in range(sft)], axis=0)
        prev[...] = last
        return jnp.concatenate([wrap, t[:tm - rows, :]], axis=0)

    level, sums = hp[...], []
    for gi, (w, prev) in enumerate(zip(POOL_WINDOWS, hprev)):
        assert w == 2 ** (gi + 1)
        level = level + shifted(level, w // 2, prev)
        sums.append(level[:, :gc])
        level = level[:, gc:]

    first = (s_idx == 0).astype(F32)
    sub8 = lax.broadcasted_iota(jnp.int32, (SUBLANES, gc), 0)
    ys = []
    for gi, w in enumerate(POOL_WINDOWS):
        own = hp[:, gi * gc:(gi + 1) * gc]
        mean = sums[gi] * (1.0 / w)
        fix = [jnp.where(sub8 == 0, 1.0 + first * (w / (k + 1.0) - 1.0), 1.0) for k in range(w - 1)]
        head = jnp.concatenate(fix, axis=0) * mean[0:(w - 1) * SUBLANES, :]
        mean = jnp.concatenate([head, mean[(w - 1) * SUBLANES:, :]], axis=0)
        d = mean - own
        ys.append(jnp.dot(d.astype(BF16), wp_ref[gi], preferred_element_type=F32))
    m = jnp.concatenate(ys, axis=-1) * scale_ref[...]
    r = _rms(m, gpost_ref[...])
    for k in range(Q):
        for j in range(nslab):
            stage[j, pl.ds(k, SUBLANES, stride=P), :] = r[k * SUBLANES:(k + 1) * SUBLANES, j * LANES:(j + 1) * LANES]
    rn = jnp.concatenate([jnp.concatenate([stage[j, i * P:i * P + Q, :] for i in range(SUBLANES)], axis=0)
                          for j in range(nslab)], axis=1)
    o_ref[0] = x + rn


def _pool_layer(x, gpre, gpost, wp, scale, win, wout, part, nparts, *, tm):
    B, S, D = x.shape
    gc = D // len(POOL_WINDOWS)
    wmax = max(POOL_WINDOWS)
    assert tm // SUBLANES >= wmax
    ns = S // tm
    nsteps = B * ns
    xspec = pl.BlockSpec((1, tm, D), lambda b, s: (b, s, 0))

    def cast_specs(w):
        rows = w.shape[0] // nparts
        blk = rows // nsteps
        assert w.shape[0] % nparts == 0 and rows % nsteps == 0 and blk % (2 * SUBLANES) == 0
        return (pl.BlockSpec((blk, w.shape[1]), lambda b, s: (part * nsteps + b * ns + s, 0)),
                pl.BlockSpec((blk, w.shape[1]), lambda b, s: (b * ns + s, 0)),
                jax.ShapeDtypeStruct((rows, w.shape[1]), BF16))

    win_in, win_out, win_shape = cast_specs(win)
    wout_in, wout_out, wout_shape = cast_specs(wout)
    return pl.pallas_call(
        functools.partial(_pool_kernel, tm=tm, gc=gc),
        grid=(B, ns),
        in_specs=[xspec, _const_spec((1, D)), _const_spec((1, D)),
                  _const_spec(wp.shape), _const_spec((1, D)), win_in, wout_in],
        out_specs=[xspec, win_out, wout_out],
        out_shape=[jax.ShapeDtypeStruct(x.shape, x.dtype), win_shape, wout_shape],
        scratch_shapes=[pltpu.VMEM((D // LANES, tm + SUBLANES * STAGE_PAD, LANES), F32),
                        pltpu.VMEM((tm, D), F32)]
                       + [pltpu.VMEM((w // 2 * SUBLANES, D - gi * gc), F32) for gi, w in enumerate(POOL_WINDOWS)],
        compiler_params=_params(),
        name="pool_layer",
    )(x, gpre, gpost, wp, scale, win, wout)


def _gelu_gate(gate, val):
    inner = gate * (GELU_C + (GELU_C * 0.044715) * (gate * gate))
    return (gate * val) * (0.5 * jnp.tanh(inner) + 0.5)


def _ffn_kernel(x_ref, gpre_ref, gpost_ref, win_ref, cw_ref, cb_ref, wout_ref, o_ref,
                stage_in, stage_out, h_scr, act_scr, halo, *, tm, fc, nc):
    F = nc * fc
    D = wout_ref.shape[1]
    Q = tm // SUBLANES
    P = Q + STAGE_PAD
    nslab = D // LANES

    @pl.when(pl.program_id(1) == 0)
    def _():
        halo[...] = jnp.zeros(halo.shape, F32)

    x = x_ref[0]
    h = _rms(x, gpre_ref[...])
    for j in range(nslab):
        for i in range(SUBLANES):
            stage_in[j, i * P:i * P + Q, :] = h[i * Q:(i + 1) * Q, j * LANES:(j + 1) * LANES]
    for k in range(0, Q, 2):
        blk = [jnp.concatenate([stage_in[j, pl.ds(k + d, SUBLANES, stride=P), :] for j in range(nslab)], axis=1)
               for d in range(2)]
        h_scr[k * SUBLANES:(k + 2) * SUBLANES, :] = jnp.concatenate(blk, axis=0).astype(BF16)

    sub = lax.broadcasted_iota(jnp.int32, (SUBLANES, fc), 0)

    def conv(col):
        u = jnp.dot(h_scr[...], win_ref[:, col:col + fc], preferred_element_type=F32)
        last1, last2 = u[tm - SUBLANES:, :], u[tm - 2 * SUBLANES:tm - SUBLANES, :]
        b1 = pltpu.roll(jnp.where(sub == SUBLANES - 1, halo[0:SUBLANES, col:col + fc], last1), 1, 0)
        b2 = pltpu.roll(jnp.where(sub == SUBLANES - 1, halo[SUBLANES:2 * SUBLANES, col:col + fc], last2), 1, 0)
        halo[0:SUBLANES, col:col + fc] = last1
        halo[SUBLANES:2 * SUBLANES, col:col + fc] = last2
        um1 = jnp.concatenate([b1, u[:tm - SUBLANES, :]], axis=0)
        um2 = jnp.concatenate([b2, b1, u[:tm - 2 * SUBLANES, :]], axis=0)
        return (cw_ref[0:1, col:col + fc] * um2 + cw_ref[1:2, col:col + fc] * um1
                + cw_ref[2:3, col:col + fc] * u + cb_ref[:, col:col + fc])

    for c in range(nc):
        gate = conv(c * fc)
        val = conv(F + c * fc)
        act_scr[:, c * fc:(c + 1) * fc] = _gelu_gate(gate, val).astype(BF16)

    f = jnp.dot(act_scr[...], wout_ref[...], preferred_element_type=F32)
    for k in range(Q):
        for j in range(nslab):
            stage_out[j, pl.ds(k, SUBLANES, stride=P), :] = f[k * SUBLANES:(k + 1) * SUBLANES, j * LANES:(j + 1) * LANES]
    fn = jnp.concatenate([jnp.concatenate([stage_out[j, i * P:i * P + Q, :] for i in range(SUBLANES)], axis=0)
                          for j in range(nslab)], axis=1)
    o_ref[0] = x + _rms(fn, gpost_ref[...])


def _ffn_layer(x, gpre, gpost, win, cw, cb, wout, *, layer, tm):
    B, S, D = x.shape
    F = wout.shape[1]
    fc = FFN_CHUNK
    nc = F // fc
    assert cw.shape == (CONV_WIDTH, 2 * F) and tm % (2 * SUBLANES) == 0
    xspec = pl.BlockSpec((1, tm, D), lambda b, s: (b, s, 0))
    stage = pltpu.VMEM((D // LANES, tm + SUBLANES * STAGE_PAD, LANES), F32)
    return pl.pallas_call(
        functools.partial(_ffn_kernel, tm=tm, fc=fc, nc=nc),
        grid=(B, S // tm),
        in_specs=[xspec, _const_spec((1, D)), _const_spec((1, D)), _layer_spec(win.shape, layer),
                  _const_spec(cw.shape), _const_spec(cb.shape), _layer_spec(wout.shape, layer)],
        out_specs=xspec,
        out_shape=jax.ShapeDtypeStruct(x.shape, x.dtype),
        scratch_shapes=[stage, stage,
                        pltpu.VMEM((tm, D), BF16), pltpu.VMEM((tm, F), BF16),
                        pltpu.VMEM(((CONV_WIDTH - 1) * SUBLANES, 2 * F), F32)],
        compiler_params=_params(),
        name="ffn_layer",
    )(x, gpre, gpost, win, cw, cb, wout)


def _rope_tables(pos, invf, sign, tm):
    ngrp = LANES // HALF
    q4 = tm // ngrp
    grp = lax.broadcasted_iota(jnp.int32, (q4, LANES), 1) // HALF
    packed = jnp.zeros((q4, LANES), F32)
    for g in range(ngrp):
        packed = jnp.where(grp == g, pos[g * q4:(g + 1) * q4, :], packed)
    ang = packed * invf
    tables = []
    for table in (jnp.cos(ang), jnp.sin(ang)):
        rolled = [table] + [pltpu.roll(table, HALF * d, 1) for d in range(1, ngrp)]
        quarters = []
        for g in range(ngrp):
            out = rolled[(0 - g) % ngrp]
            for h in range(1, ngrp):
                out = jnp.where(grp == h, rolled[(h - g) % ngrp], out)
            quarters.append(out)
        tables.append(jnp.concatenate(quarters, axis=0))
    return tables[0], tables[1] * sign


def _rope(t, cos, sin):
    return t * cos + pltpu.roll(t, LANES // 2, 1) * sin


def _kv_kernel(x_ref, g_ref, pos_ref, invf_ref, sign_ref, wk_ref, wv_ref, k_ref, v_ref, cos_ref, sin_ref, *, tm):
    cos, sin = _rope_tables(pos_ref[0].astype(F32), invf_ref[...], sign_ref[...], tm)
    cos_ref[0] = cos
    sin_ref[0] = sin
    h = _rms(x_ref[0], g_ref[...]).astype(BF16)
    kd = jnp.dot(h, wk_ref[...], preferred_element_type=F32)
    for j in range(N_KV_HEADS):
        k_ref[0, :, j * LANES:(j + 1) * LANES] = _rope(kd[:, j * LANES:(j + 1) * LANES], cos, sin).astype(BF16)
    vd = jnp.dot(h, wv_ref[...], preferred_element_type=F32).astype(BF16)
    for j in range(N_KV_HEADS):
        v_ref[0, :, 2 * j * LANES:(2 * j + 1) * LANES] = vd[:, j * LANES:(j + 1) * LANES]
        v_ref[0, :, (2 * j + 1) * LANES:(2 * j + 2) * LANES] = jnp.ones((tm, LANES), BF16)


def _kv_proj(x, g, positions, wk, wv, *, tm):
    B, S, D = x.shape
    W = N_KV_HEADS * LANES
    inv_freq = 1.0 / (ROPE_THETA ** (jnp.arange(0, HEAD_DIM, 2, dtype=F32) / HEAD_DIM))
    invf = jnp.tile(inv_freq, LANES // HALF)[None, :]
    sign = jnp.where(jnp.arange(LANES) < LANES // 2, -1.0, 1.0).astype(F32)[None, :]
    tspec = pl.BlockSpec((1, tm, LANES), lambda b, s: (b, s, 0))
    ospec = pl.BlockSpec((1, tm, W), lambda b, s: (b, s, 0))
    return pl.pallas_call(
        functools.partial(_kv_kernel, tm=tm),
        grid=(B, S // tm),
        in_specs=[pl.BlockSpec((1, tm, D), lambda b, s: (b, s, 0)), _const_spec((1, D)),
                  pl.BlockSpec((1, tm, 1), lambda b, s: (b, s, 0)), _const_spec((1, LANES)), _const_spec((1, LANES)),
                  _const_spec(wk.shape), _const_spec(wv.shape)],
        out_specs=[ospec, pl.BlockSpec((1, tm, 2 * W), lambda b, s: (b, s, 0)), tspec, tspec],
        out_shape=[jax.ShapeDtypeStruct((B, S, W), BF16), jax.ShapeDtypeStruct((B, S, 2 * W), BF16)]
                  + [jax.ShapeDtypeStruct((B, S, LANES), F32)] * 2,
        compiler_params=_params(),
        name="kv_proj",
    )(x, g, positions[:, :, None], invf, sign, wk, wv)


def _attn_kernel(sinks_ref, x_ref, gpre_ref, gpost_ref, cos_ref, sin_ref, kc_ref, kp_ref, vc_ref, vp_ref,
                 wq_ref, wo_ref, o_ref, qa, qb, o_scr, *, tm):
    s_idx = pl.program_id(1)
    nblk = tm // BLOCK
    D = wq_ref.shape[0]
    cw = D // N_KV_HEADS
    lane = lax.broadcasted_iota(jnp.int32, (1, LANES), 1)
    first_of_pair = (lane % HEAD_DIM) < HALF

    def pre_norm(r0):
        return _rms(x_ref[0, pl.ds(r0, BLOCK), :], gpre_ref[...]).astype(BF16)

    def q_piece(h, j, r0, qdst):
        q = jnp.dot(h, wq_ref[:, j * cw:(j + 1) * cw], preferred_element_type=F32)
        cos, sin = cos_ref[0, pl.ds(r0, BLOCK), :], sin_ref[0, pl.ds(r0, BLOCK), :]
        for p in range(cw // LANES):
            qr = _rope(q[:, p * LANES:(p + 1) * LANES], cos, sin) * (ATTN_SCALE * LOG2E)
            qdst[Q_PER_KV * j + 2 * p] = jnp.where(first_of_pair, qr, 0.0).astype(BF16)
            qdst[Q_PER_KV * j + 2 * p + 1] = jnp.where(first_of_pair, 0.0, qr).astype(BF16)

    def post_norm(mo_parts, r0):
        mo = jnp.concatenate(mo_parts, axis=-1)
        o_ref[0, pl.ds(r0, BLOCK), :] = x_ref[0, pl.ds(r0, BLOCK), :] + _rms(mo, gpost_ref[...])

    def key_rows(cur_ref, prev_ref, r0, has_prev, cols):
        if has_prev:
            return cur_ref[0, pl.ds(pl.multiple_of(r0 - BLOCK, BLOCK), 2 * BLOCK), cols]
        return jnp.concatenate([prev_ref[0, :, cols], cur_ref[0, 0:BLOCK, cols]], axis=0)

    qi = lax.broadcasted_iota(jnp.int32, (BLOCK, 2 * BLOCK), 0)
    kj = lax.broadcasted_iota(jnp.int32, (BLOCK, 2 * BLOCK), 1)
    rel = BLOCK + qi - kj
    band = (rel >= 0) & (rel < WINDOW)
    out_lo = lax.broadcasted_iota(jnp.int32, (BLOCK, LANES), 1) < HEAD_DIM

    def block_step(jb, qcur, qnext, has_prev=True, has_next=True):
        r0 = pl.multiple_of(jb * BLOCK, BLOCK)
        rp = pl.multiple_of((jb - 1) * BLOCK, BLOCK) if has_prev else None
        rn = pl.multiple_of((jb + 1) * BLOCK, BLOCK) if has_next else None
        o_prev = o_scr[pl.ds(rp, BLOCK), :] if has_prev else None
        h_next = pre_norm(rn) if has_next else None
        seq_start = (s_idx * tm + r0 == 0).astype(jnp.int32)
        valid = band & (kj >= BLOCK * seq_start)
        mo_parts = []

        def scores(j):
            qblk = jnp.concatenate([qcur[Q_PER_KV * j + i] for i in range(Q_PER_KV)], axis=0)
            kblk = key_rows(kc_ref, kp_ref, r0, has_prev, slice(j * LANES, (j + 1) * LANES))
            return lax.dot_general(qblk, kblk, (((1,), (1,)), ((), ())), preferred_element_type=F32)

        def projections(j):
            if has_prev:
                mo_parts.append(jnp.dot(o_prev, wo_ref[:, j * cw:(j + 1) * cw], preferred_element_type=F32))
            if has_next:
                q_piece(h_next, j, rn, qnext)

        def values(j, sc):
            ps, sink_p = [], []
            for i in range(Q_PER_KV):
                sink = sinks_ref[Q_PER_KV * j + i] * LOG2E
                sh = jnp.where(valid, sc[i * BLOCK:(i + 1) * BLOCK], NEG_INF)
                m = jnp.maximum(jnp.max(sh, axis=-1, keepdims=True), sink)
                ps.append(jnp.exp2(sh - m).astype(BF16))
                sink_p.append(jnp.exp2(sink - m))
            vblk = key_rows(vc_ref, vp_ref, r0, has_prev, slice(2 * j * LANES, (2 * j + 2) * LANES))
            ov = jnp.dot(jnp.concatenate(ps, axis=0), vblk, preferred_element_type=F32)
            heads = []
            for i in range(Q_PER_KV):
                num = ov[i * BLOCK:(i + 1) * BLOCK, :LANES]
                den = ov[i * BLOCK:(i + 1) * BLOCK, LANES:] + sink_p[i]
                heads.append(num / den)
            for pp in range(Q_PER_KV // 2):
                col = (Q_PER_KV // 2 * j + pp) * LANES
                o_scr[pl.ds(r0, BLOCK), col:col + LANES] = jnp.where(out_lo, heads[2 * pp], heads[2 * pp + 1]).astype(BF16)

        sc = scores(0)
        for j in range(N_KV_HEADS):
            sc_next = scores(j + 1) if j + 1 < N_KV_HEADS else None
            projections(j)
            values(j, sc)
            sc = sc_next
        if has_prev:
            post_norm(mo_parts, rp)

    h0 = pre_norm(0)
    for j in range(N_KV_HEADS):
        q_piece(h0, j, 0, qa)

    block_step(0, qa, qb, has_prev=False)

    def pair_body(i, carry):
        block_step(2 * i + 1, qb, qa)
        block_step(2 * i + 2, qa, qb)
        return carry

    lax.fori_loop(0, nblk // 2 - 1, pair_body, 0)
    block_step(nblk - 1, qb, qa, has_next=False)
    r_last = (nblk - 1) * BLOCK
    mo_last = jnp.dot(o_scr[r_last:r_last + BLOCK, :], wo_ref[...], preferred_element_type=F32)
    o_ref[0, r_last:r_last + BLOCK, :] = x_ref[0, r_last:r_last + BLOCK, :] + _rms(mo_last, gpost_ref[...])


def _attn_layer(x, gpre, gpost, cos, sin, k, v, wq, wo, sinks, *, layer, tm):
    B, S, D = x.shape
    W = N_KV_HEADS * LANES
    nbt = tm // BLOCK
    xspec = pl.BlockSpec((1, tm, D), lambda b, s: (b, s, 0))
    tspec = pl.BlockSpec((1, tm, LANES), lambda b, s: (b, s, 0))

    def cur(width):
        return pl.BlockSpec((1, tm, width), lambda b, s: (b, s, 0))

    def prev(width):
        return pl.BlockSpec((1, BLOCK, width), lambda b, s: (b, jnp.maximum(s * nbt - 1, 0), 0))

    return pl.pallas_call(
        functools.partial(_attn_kernel, tm=tm),
        grid=(B, S // tm),
        in_specs=[pl.BlockSpec(memory_space=pltpu.SMEM), xspec, _const_spec((1, D)), _const_spec((1, D)),
                  tspec, tspec, cur(W), prev(W), cur(2 * W), prev(2 * W),
                  _layer_spec(wq.shape, layer), _layer_spec(wo.shape, layer)],
        out_specs=xspec,
        out_shape=jax.ShapeDtypeStruct(x.shape, x.dtype),
        scratch_shapes=[pltpu.VMEM((N_Q_HEADS, BLOCK, LANES), BF16), pltpu.VMEM((N_Q_HEADS, BLOCK, LANES), BF16),
                        pltpu.VMEM((tm, D), BF16)],
        compiler_params=_params(),
        name="attn_layer",
    )(sinks, x, gpre, gpost, cos, sin, k, k, v, v, wq, wo)


def _pair_layout_q(wq):
    lead = wq.shape[:-1]
    w = wq.reshape(*lead, N_Q_HEADS // 2, 2, 2, HALF)
    w = jnp.swapaxes(w, -3, -2)
    return w.reshape(*lead, N_Q_HEADS * HEAD_DIM)


def _dup_layout_k(wk):
    D = wk.shape[0]
    w = wk.reshape(D, N_KV_HEADS, 2, 1, HALF)
    w = jnp.broadcast_to(w, (D, N_KV_HEADS, 2, 2, HALF))
    return w.reshape(D, N_KV_HEADS * LANES)


def _dup_layout_v(wv):
    D = wv.shape[0]
    w = wv.reshape(D, N_KV_HEADS, 1, HEAD_DIM)
    w = jnp.broadcast_to(w, (D, N_KV_HEADS, 2, HEAD_DIM))
    return w.reshape(D, N_KV_HEADS * LANES)


def kernel(x, positions, mix_pre_g, mix_post_g, pool_w, pool_scale, kv_norm_g, w_kv, w_q, w_o, sinks,
           ffn_pre_g, ffn_post_g, ffn_w_in, ffn_conv_w, ffn_conv_b, ffn_w_out):
    B, S, D = x.shape
    tm = min(TILE_ROWS, S)
    depth = mix_pre_g.shape[0]
    n_a = pool_w.shape[0]
    F = ffn_w_out.shape[1]
    hkvd = N_KV_HEADS * HEAD_DIM
    assert S % tm == 0 and tm % (2 * BLOCK) == 0 and F % FFN_CHUNK == 0

    assert n_a >= 1 and depth % n_a == 0
    per = depth // n_a
    win2d, wout2d = ffn_w_in.reshape(depth * D, 2 * F), ffn_w_out.reshape(depth * F, D)
    win_bf, wout_bf = [], []
    wq_all, wo_all = _pair_layout_q(w_q).astype(BF16), w_o.astype(BF16)
    cos = sin = k = v = None
    for layer in range(depth):
        gpre, gpost = mix_pre_g[layer][None, :], mix_post_g[layer][None, :]
        if layer < n_a:
            x, wi, wo_ = _pool_layer(x, gpre, gpost, pool_w[layer].astype(BF16), pool_scale[layer][None, :],
                                     win2d, wout2d, layer, n_a, tm=tm)
            win_bf.append(wi.reshape(per, D, 2 * F))
            wout_bf.append(wo_.reshape(per, F, D))
        else:
            if layer == n_a:
                k, v, cos, sin = _kv_proj(x, kv_norm_g[None, :], positions,
                                          _dup_layout_k(w_kv[:, :hkvd]).astype(BF16),
                                          _dup_layout_v(w_kv[:, hkvd:]).astype(BF16), tm=tm)
            j = layer - n_a
            x = _attn_layer(x, gpre, gpost, cos, sin, k, v, wq_all, wo_all, sinks[j], layer=j, tm=tm)
        x = _ffn_layer(x, ffn_pre_g[layer][None, :], ffn_post_g[layer][None, :],
                       win_bf[layer // per], ffn_conv_w[layer], ffn_conv_b[layer][None, :], wout_bf[layer // per],
                       layer=layer % per, tm=tm)
    return x
```

```python
import functools
import math

import jax
import jax.numpy as jnp
from jax import lax
from jax.experimental import pallas as pl
from jax.experimental.pallas import tpu as pltpu

F32 = jnp.float32
BF16 = jnp.bfloat16

RMS_EPS = 1e-6
POOL_WINDOWS = (2, 4, 8, 16)
HEAD_DIM = 64
HALF = HEAD_DIM // 2
N_Q_HEADS = 16
N_KV_HEADS = 4
Q_PER_KV = N_Q_HEADS // N_KV_HEADS
WINDOW = 128
BLOCK = 128
ROPE_THETA = 10000.0
ATTN_SCALE = 1.0 / math.sqrt(HEAD_DIM)
NEG_INF = -1e30
CONV_WIDTH = 3
LANES = 128
SUBLANES = 8
STAGE_PAD = 8
FFN_CHUNK = 256
VMEM_LIMIT = 56 * 1024 * 1024

TILE_ROWS = 1024

GELU_C = math.sqrt(2.0 / math.pi)
LOG2E = math.log2(math.e)


def _rms(x, g):
    ms = jnp.mean(x * x, axis=-1, keepdims=True)
    return x * lax.rsqrt(ms + RMS_EPS) * g


def _const_spec(shape):
    nd = len(shape)
    return pl.BlockSpec(shape, lambda b, s: (0,) * nd, pipeline_mode=pl.Buffered(1))


def _layer_spec(shape, layer):
    nd = len(shape) - 1
    return pl.BlockSpec((None,) + tuple(shape[1:]), lambda b, s: (layer,) + (0,) * nd, pipeline_mode=pl.Buffered(1))


def _params():
    return pltpu.CompilerParams(dimension_semantics=("arbitrary", "arbitrary"),
                                vmem_limit_bytes=VMEM_LIMIT)


def _pool_kernel(x_ref, gpre_ref, gpost_ref, wp_ref, scale_ref, win_ref, wout_ref, o_ref, win_bf_ref, wout_bf_ref,
                 stage, hp, *hprev, tm, gc):
    win_bf_ref[...] = win_ref[...].astype(BF16)
    wout_bf_ref[...] = wout_ref[...].astype(BF16)
    s_idx = pl.program_id(1)
    D = hp.shape[1]
    Q = tm // SUBLANES
    P = Q + STAGE_PAD
    nslab = D // LANES

    @pl.when(s_idx == 0)
    def _():
        for prev in hprev:
            prev[...] = jnp.zeros(prev.shape, F32)

    x = x_ref[0]
    h = _rms(x, gpre_ref[...])
    for j in range(nslab):
        for i in range(SUBLANES):
            stage[j, i * P:i * P + Q, :] = h[i * Q:(i + 1) * Q, j * LANES:(j + 1) * LANES]
    for k in range(Q):
        hp[k * SUBLANES:(k + 1) * SUBLANES, :] = jnp.concatenate(
            [stage[j, pl.ds(k, SUBLANES, stride=P), :] for j in range(nslab)], axis=1)

    def shifted(t, sft, prev):
        rows, cols = sft * SUBLANES, t.shape[1]
        last = t[tm - rows:, :]
        sub = lax.broadcasted_iota(jnp.int32, (rows, cols), 0) % SUBLANES
        wrap = jnp.where(sub == SUBLANES - 1, prev[...], last)
        wrap = jnp.concatenate([pltpu.roll(wrap[m * SUBLANES:(m + 1) * SUBLANES, :], 1, 0) for m in range(sft)], axis=0)
        prev[...] = last
        return jnp.concatenate([wrap, t[:tm - rows, :]], axis=0)

    level, sums = hp[...], []
    for gi, (w, prev) in enumerate(zip(POOL_WINDOWS, hprev)):
        assert w == 2 ** (gi + 1)
        level = level + shifted(level, w // 2, prev)
        sums.append(level[:, :gc])
        level = level[:, gc:]

    first = (s_idx == 0).astype(F32)
    sub8 = lax.broadcasted_iota(jnp.int32, (SUBLANES, gc), 0)
    ys = []
    for gi, w in enumerate(POOL_WINDOWS):
        own = hp[:, gi * gc:(gi + 1) * gc]
        mean = sums[gi] * (1.0 / w)
        fix = [jnp.where(sub8 == 0, 1.0 + first * (w / (k + 1.0) - 1.0), 1.0) for k in range(w - 1)]
        head = jnp.concatenate(fix, axis=0) * mean[0:(w - 1) * SUBLANES, :]
        mean = jnp.concatenate([head, mean[(w - 1) * SUBLANES:, :]], axis=0)
        d = mean - own
        ys.append(jnp.dot(d.astype(BF16), wp_ref[gi], preferred_element_type=F32))
    m = jnp.concatenate(ys, axis=-1) * scale_ref[...]
    r = _rms(m, gpost_ref[...])
    for k in range(Q):
        for j in range(nslab):
            stage[j, pl.ds(k, SUBLANES, stride=P), :] = r[k * SUBLANES:(k + 1) * SUBLANES, j * LANES:(j + 1) * LANES]
    rn = jnp.concatenate([jnp.concatenate([stage[j, i * P:i * P + Q, :] for i in range(SUBLANES)], axis=0)
                          for j in range(nslab)], axis=1)
    o_ref[0] = x + rn


def _pool_layer(x, gpre, gpost, wp, scale, win, wout, part, nparts, *, tm):
    B, S, D = x.shape
    gc = D // len(POOL_WINDOWS)
    wmax = max(POOL_WINDOWS)
    assert tm // SUBLANES >= wmax
    ns = S // tm
    nsteps = B * ns
    xspec = pl.BlockSpec((1, tm, D), lambda b, s: (b, s, 0))

    def cast_specs(w):
        rows = w.shape[0] // nparts
        blk = rows // nsteps
        assert w.shape[0] % nparts == 0 and rows % nsteps == 0 and blk % (2 * SUBLANES) == 0
        return (pl.BlockSpec((blk, w.shape[1]), lambda b, s: (part * nsteps + b * ns + s, 0)),
                pl.BlockSpec((blk, w.shape[1]), lambda b, s: (b * ns + s, 0)),
                jax.ShapeDtypeStruct((rows, w.shape[1]), BF16))

    win_in, win_out, win_shape = cast_specs(win)
    wout_in, wout_out, wout_shape = cast_specs(wout)
    return pl.pallas_call(
        functools.partial(_pool_kernel, tm=tm, gc=gc),
        grid=(B, ns),
        in_specs=[xspec, _const_spec((1, D)), _const_spec((1, D)),
                  _const_spec(wp.shape), _const_spec((1, D)), win_in, wout_in],
        out_specs=[xspec, win_out, wout_out],
        out_shape=[jax.ShapeDtypeStruct(x.shape, x.dtype), win_shape, wout_shape],
        scratch_shapes=[pltpu.VMEM((D // LANES, tm + SUBLANES * STAGE_PAD, LANES), F32),
                        pltpu.VMEM((tm, D), F32)]
                       + [pltpu.VMEM((w // 2 * SUBLANES, D - gi * gc), F32) for gi, w in enumerate(POOL_WINDOWS)],
        compiler_params=_params(),
        name="pool_layer",
    )(x, gpre, gpost, wp, scale, win, wout)


def _gelu_gate(gate, val):
    inner = gate * (GELU_C + (GELU_C * 0.044715) * (gate * gate))
    return (gate * val) * (0.5 * jnp.tanh(inner) + 0.5)


def _ffn_kernel(x_ref, gpre_ref, gpost_ref, win_ref, cw_ref, cb_ref, wout_ref, o_ref,
                stage_in, stage_out, h_scr, act_scr, halo, *, tm, fc, nc):
    F = nc * fc
    D = wout_ref.shape[1]
    Q = tm // SUBLANES
    P = Q + STAGE_PAD
    nslab = D // LANES

    @pl.when(pl.program_id(1) == 0)
    def _():
        halo[...] = jnp.zeros(halo.shape, F32)

    x = x_ref[0]
    h = _rms(x, gpre_ref[...])
    for j in range(nslab):
        for i in range(SUBLANES):
            stage_in[j, i * P:i * P + Q, :] = h[i * Q:(i + 1) * Q, j * LANES:(j + 1) * LANES]
    for k in range(0, Q, 2):
        blk = [jnp.concatenate([stage_in[j, pl.ds(k + d, SUBLANES, stride=P), :] for j in range(nslab)], axis=1)
               for d in range(2)]
        h_scr[k * SUBLANES:(k + 2) * SUBLANES, :] = jnp.concatenate(blk, axis=0).astype(BF16)

    sub = lax.broadcasted_iota(jnp.int32, (SUBLANES, fc), 0)

    def conv(col):
        u = jnp.dot(h_scr[...], win_ref[:, col:col + fc], preferred_element_type=F32)
        last1, last2 = u[tm - SUBLANES:, :], u[tm - 2 * SUBLANES:tm - SUBLANES, :]
        b1 = pltpu.roll(jnp.where(sub == SUBLANES - 1, halo[0:SUBLANES, col:col + fc], last1), 1, 0)
        b2 = pltpu.roll(jnp.where(sub == SUBLANES - 1, halo[SUBLANES:2 * SUBLANES, col:col + fc], last2), 1, 0)
        halo[0:SUBLANES, col:col + fc] = last1
        halo[SUBLANES:2 * SUBLANES, col:col + fc] = last2
        um1 = jnp.concatenate([b1, u[:tm - SUBLANES, :]], axis=0)
        um2 = jnp.concatenate([b2, b1, u[:tm - 2 * SUBLANES, :]], axis=0)
        return (cw_ref[0:1, col:col + fc] * um2 + cw_ref[1:2, col:col + fc] * um1
                + cw_ref[2:3, col:col + fc] * u + cb_ref[:, col:col + fc])

    for c in range(nc):
        gate = conv(c * fc)
        val = conv(F + c * fc)
        act_scr[:, c * fc:(c + 1) * fc] = _gelu_gate(gate, val).astype(BF16)

    f = jnp.dot(act_scr[...], wout_ref[...], preferred_element_type=F32)
    for k in range(Q):
        for j in range(nslab):
            stage_out[j, pl.ds(k, SUBLANES, stride=P), :] = f[k * SUBLANES:(k + 1) * SUBLANES, j * LANES:(j + 1) * LANES]
    fn = jnp.concatenate([jnp.concatenate([stage_out[j, i * P:i * P + Q, :] for i in range(SUBLANES)], axis=0)
                          for j in range(nslab)], axis=1)
    o_ref[0] = x + _rms(fn, gpost_ref[...])


def _ffn_layer(x, gpre, gpost, win, cw, cb, wout, *, layer, tm):
    B, S, D = x.shape
    F = wout.shape[1]
    fc = FFN_CHUNK
    nc = F // fc
    assert cw.shape == (CONV_WIDTH, 2 * F) and tm % (2 * SUBLANES) == 0
    xspec = pl.BlockSpec((1, tm, D), lambda b, s: (b, s, 0))
    stage = pltpu.VMEM((D // LANES, tm + SUBLANES * STAGE_PAD, LANES), F32)
    return pl.pallas_call(
        functools.partial(_ffn_kernel, tm=tm, fc=fc, nc=nc),
        grid=(B, S // tm),
        in_specs=[xspec, _const_spec((1, D)), _const_spec((1, D)), _layer_spec(win.shape, layer),
                  _const_spec(cw.shape), _const_spec(cb.shape), _layer_spec(wout.shape, layer)],
        out_specs=xspec,
        out_shape=jax.ShapeDtypeStruct(x.shape, x.dtype),
        scratch_shapes=[stage, stage,
                        pltpu.VMEM((tm, D), BF16), pltpu.VMEM((tm, F), BF16),
                        pltpu.VMEM(((CONV_WIDTH - 1) * SUBLANES, 2 * F), F32)],
        compiler_params=_params(),
        name="ffn_layer",
    )(x, gpre, gpost, win, cw, cb, wout)


def _rope_tables(pos, invf, sign, tm):
    ngrp = LANES // HALF
    q4 = tm // ngrp
    grp = lax.broadcasted_iota(jnp.int32, (q4, LANES), 1) // HALF
    packed = jnp.zeros((q4, LANES), F32)
    for g in range(ngrp):
        packed = jnp.where(grp == g, pos[g * q4:(g + 1) * q4, :], packed)
    ang = packed * invf
    tables = []
    for table in (jnp.cos(ang), jnp.sin(ang)):
        rolled = [table] + [pltpu.roll(table, HALF * d, 1) for d in range(1, ngrp)]
        quarters = []
        for g in range(ngrp):
            out = rolled[(0 - g) % ngrp]
            for h in range(1, ngrp):
                out = jnp.where(grp == h, rolled[(h - g) % ngrp], out)
            quarters.append(out)
        tables.append(jnp.concatenate(quarters, axis=0))
    return tables[0], tables[1] * sign


def _rope(t, cos, sin):
    return t * cos + pltpu.roll(t, LANES // 2, 1) * sin


def _kv_kernel(x_ref, g_ref, pos_ref, invf_ref, sign_ref, wk_ref, wv_ref, k_ref, v_ref, cos_ref, sin_ref, *, tm):
    cos, sin = _rope_tables(pos_ref[0].astype(F32), invf_ref[...], sign_ref[...], tm)
    cos_ref[0] = cos
    sin_ref[0] = sin
    h = _rms(x_ref[0], g_ref[...]).astype(BF16)
    kd = jnp.dot(h, wk_ref[...], preferred_element_type=F32)
    for j in range(N_KV_HEADS):
        k_ref[0, :, j * LANES:(j + 1) * LANES] = _rope(kd[:, j * LANES:(j + 1) * LANES], cos, sin).astype(BF16)
    vd = jnp.dot(h, wv_ref[...], preferred_element_type=F32).astype(BF16)
    for j in range(N_KV_HEADS):
        v_ref[0, :, 2 * j * LANES:(2 * j + 1) * LANES] = vd[:, j * LANES:(j + 1) * LANES]
        v_ref[0, :, (2 * j + 1) * LANES:(2 * j + 2) * LANES] = jnp.ones((tm, LANES), BF16)


def _kv_proj(x, g, positions, wk, wv, *, tm):
    B, S, D = x.shape
    W = N_KV_HEADS * LANES
    inv_freq = 1.0 / (ROPE_THETA ** (jnp.arange(0, HEAD_DIM, 2, dtype=F32) / HEAD_DIM))
    invf = jnp.tile(inv_freq, LANES // HALF)[None, :]
    sign = jnp.where(jnp.arange(LANES) < LANES // 2, -1.0, 1.0).astype(F32)[None, :]
    tspec = pl.BlockSpec((1, tm, LANES), lambda b, s: (b, s, 0))
    ospec = pl.BlockSpec((1, tm, W), lambda b, s: (b, s, 0))
    return pl.pallas_call(
        functools.partial(_kv_kernel, tm=tm),
        grid=(B, S // tm),
        in_specs=[pl.BlockSpec((1, tm, D), lambda b, s: (b, s, 0)), _const_spec((1, D)),
                  pl.BlockSpec((1, tm, 1), lambda b, s: (b, s, 0)), _const_spec((1, LANES)), _const_spec((1, LANES)),
                  _const_spec(wk.shape), _const_spec(wv.shape)],
        out_specs=[ospec, pl.BlockSpec((1, tm, 2 * W), lambda b, s: (b, s, 0)), tspec, tspec],
        out_shape=[jax.ShapeDtypeStruct((B, S, W), BF16), jax.ShapeDtypeStruct((B, S, 2 * W), BF16)]
                  + [jax.ShapeDtypeStruct((B, S, LANES), F32)] * 2,
        compiler_params=_params(),
        name="kv_proj",
    )(x, g, positions[:, :, None], invf, sign, wk, wv)


def _attn_kernel(sinks_ref, x_ref, gpre_ref, gpost_ref, cos_ref, sin_ref, kc_ref, kp_ref, vc_ref, vp_ref,
                 wq_ref, wo_ref, o_ref, qa, qb, o_scr, *, tm):
    s_idx = pl.program_id(1)
    nblk = tm // BLOCK
    D = wq_ref.shape[0]
    cw = D // N_KV_HEADS
    lane = lax.broadcasted_iota(jnp.int32, (1, LANES), 1)
    first_of_pair = (lane % HEAD_DIM) < HALF

    def pre_norm(r0, rows=BLOCK):
        return _rms(x_ref[0, pl.ds(r0, rows), :], gpre_ref[...]).astype(BF16)

    def q_piece(h, j, r0, qdsts):
        q = jnp.dot(h, wq_ref[:, j * cw:(j + 1) * cw], preferred_element_type=F32)
        for d, qdst in enumerate(qdsts):
            rd = r0 + d * BLOCK
            cos, sin = cos_ref[0, pl.ds(rd, BLOCK), :], sin_ref[0, pl.ds(rd, BLOCK), :]
            for p in range(cw // LANES):
                qr = _rope(q[d * BLOCK:(d + 1) * BLOCK, p * LANES:(p + 1) * LANES], cos, sin) * (ATTN_SCALE * LOG2E)
                qdst[Q_PER_KV * j + 2 * p] = jnp.where(first_of_pair, qr, 0.0).astype(BF16)
                qdst[Q_PER_KV * j + 2 * p + 1] = jnp.where(first_of_pair, 0.0, qr).astype(BF16)

    def post_norm(mo_parts, r0, rows):
        mo = jnp.concatenate(mo_parts, axis=-1)
        o_ref[0, pl.ds(r0, rows), :] = x_ref[0, pl.ds(r0, rows), :] + _rms(mo, gpost_ref[...])

    def key_rows(cur_ref, prev_ref, r0, has_prev, cols):
        if has_prev:
            return cur_ref[0, pl.ds(pl.multiple_of(r0 - BLOCK, BLOCK), 2 * BLOCK), cols]
        return jnp.concatenate([prev_ref[0, :, cols], cur_ref[0, 0:BLOCK, cols]], axis=0)

    qi = lax.broadcasted_iota(jnp.int32, (BLOCK, 2 * BLOCK), 0)
    kj = lax.broadcasted_iota(jnp.int32, (BLOCK, 2 * BLOCK), 1)
    rel = BLOCK + qi - kj
    band = (rel >= 0) & (rel < WINDOW)
    out_lo = lax.broadcasted_iota(jnp.int32, (BLOCK, LANES), 1) < HEAD_DIM

    def block_step(jb, qcur, qnexts=(), n_out=0, has_prev=True):
        r0 = pl.multiple_of(jb * BLOCK, BLOCK)
        rp = pl.multiple_of((jb - n_out) * BLOCK, BLOCK) if n_out else None
        rn = pl.multiple_of((jb + 1) * BLOCK, BLOCK) if qnexts else None
        o_prev = o_scr[pl.ds(rp, n_out * BLOCK), :] if n_out else None
        h_next = pre_norm(rn, len(qnexts) * BLOCK) if qnexts else None
        seq_start = (s_idx * tm + r0 == 0).astype(jnp.int32)
        valid = band & (kj >= BLOCK * seq_start)
        mo_parts = []

        def scores(j):
            qblk = jnp.concatenate([qcur[Q_PER_KV * j + i] for i in range(Q_PER_KV)], axis=0)
            kblk = key_rows(kc_ref, kp_ref, r0, has_prev, slice(j * LANES, (j + 1) * LANES))
            return lax.dot_general(qblk, kblk, (((1,), (1,)), ((), ())), preferred_element_type=F32)

        def projections(j):
            if n_out:
                mo_parts.append(jnp.dot(o_prev, wo_ref[:, j * cw:(j + 1) * cw], preferred_element_type=F32))
            if qnexts:
                q_piece(h_next, j, rn, qnexts)

        def values(j, sc):
            ps, sink_p = [], []
            for i in range(Q_PER_KV):
                sink = sinks_ref[Q_PER_KV * j + i] * LOG2E
                sh = jnp.where(valid, sc[i * BLOCK:(i + 1) * BLOCK], NEG_INF)
                m = jnp.maximum(jnp.max(sh, axis=-1, keepdims=True), sink)
                ps.append(jnp.exp2(sh - m).astype(BF16))
                sink_p.append(jnp.exp2(sink - m))
            vblk = key_rows(vc_ref, vp_ref, r0, has_prev, slice(2 * j * LANES, (2 * j + 2) * LANES))
            ov = jnp.dot(jnp.concatenate(ps, axis=0), vblk, preferred_element_type=F32)
            heads = []
            for i in range(Q_PER_KV):
                num = ov[i * BLOCK:(i + 1) * BLOCK, :LANES]
                den = ov[i * BLOCK:(i + 1) * BLOCK, LANES:] + sink_p[i]
                heads.append(num / den)
            for pp in range(Q_PER_KV // 2):
                col = (Q_PER_KV // 2 * j + pp) * LANES
                o_scr[pl.ds(r0, BLOCK), col:col + LANES] = jnp.where(out_lo, heads[2 * pp], heads[2 * pp + 1]).astype(BF16)

        sc = scores(0)
        for j in range(N_KV_HEADS):
            sc_next = scores(j + 1) if j + 1 < N_KV_HEADS else None
            projections(j)
            values(j, sc)
            sc = sc_next
        if n_out:
            post_norm(mo_parts, rp, n_out * BLOCK)

    h0 = pre_norm(0)
    for j in range(N_KV_HEADS):
        q_piece(h0, j, 0, (qa,))

    block_step(0, qa, qnexts=(qb,), has_prev=False)

    def pair_body(i, carry):
        block_step(2 * i + 1, qb, qnexts=(qa, qb))
        block_step(2 * i + 2, qa, n_out=2)
        return carry

    lax.fori_loop(0, nblk // 2 - 1, pair_body, 0)
    block_step(nblk - 1, qb, n_out=1)
    r_last = (nblk - 1) * BLOCK
    mo_last = jnp.dot(o_scr[r_last:r_last + BLOCK, :], wo_ref[...], preferred_element_type=F32)
    o_ref[0, r_last:r_last + BLOCK, :] = x_ref[0, r_last:r_last + BLOCK, :] + _rms(mo_last, gpost_ref[...])


def _attn_layer(x, gpre, gpost, cos, sin, k, v, wq, wo, sinks, *, layer, tm):
    B, S, D = x.shape
    W = N_KV_HEADS * LANES
    nbt = tm // BLOCK
    xspec = pl.BlockSpec((1, tm, D), lambda b, s: (b, s, 0))
    tspec = pl.BlockSpec((1, tm, LANES), lambda b, s: (b, s, 0))

    def cur(width):
        return pl.BlockSpec((1, tm, width), lambda b, s: (b, s, 0))

    def prev(width):
        return pl.BlockSpec((1, BLOCK, width), lambda b, s: (b, jnp.maximum(s * nbt - 1, 0), 0))

    return pl.pallas_call(
        functools.partial(_attn_kernel, tm=tm),
        grid=(B, S // tm),
        in_specs=[pl.BlockSpec(memory_space=pltpu.SMEM), xspec, _const_spec((1, D)), _const_spec((1, D)),
                  tspec, tspec, cur(W), prev(W), cur(2 * W), prev(2 * W),
                  _layer_spec(wq.shape, layer), _layer_spec(wo.shape, layer)],
        out_specs=xspec,
        out_shape=jax.ShapeDtypeStruct(x.shape, x.dtype),
        scratch_shapes=[pltpu.VMEM((N_Q_HEADS, BLOCK, LANES), BF16), pltpu.VMEM((N_Q_HEADS, BLOCK, LANES), BF16),
                        pltpu.VMEM((tm, D), BF16)],
        compiler_params=_params(),
        name="attn_layer",
    )(sinks, x, gpre, gpost, cos, sin, k, k, v, v, wq, wo)


def _pair_layout_q(wq):
    lead = wq.shape[:-1]
    w = wq.reshape(*lead, N_Q_HEADS // 2, 2, 2, HALF)
    w = jnp.swapaxes(w, -3, -2)
    return w.reshape(*lead, N_Q_HEADS * HEAD_DIM)


def _dup_layout_k(wk):
    D = wk.shape[0]
    w = wk.reshape(D, N_KV_HEADS, 2, 1, HALF)
    w = jnp.broadcast_to(w, (D, N_KV_HEADS, 2, 2, HALF))
    return w.reshape(D, N_KV_HEADS * LANES)


def _dup_layout_v(wv):
    D = wv.shape[0]
    w = wv.reshape(D, N_KV_HEADS, 1, HEAD_DIM)
    w = jnp.broadcast_to(w, (D, N_KV_HEADS, 2, HEAD_DIM))
    return w.reshape(D, N_KV_HEADS * LANES)


def kernel(x, positions, mix_pre_g, mix_post_g, pool_w, pool_scale, kv_norm_g, w_kv, w_q, w_o, sinks,
           ffn_pre_g, ffn_post_g, ffn_w_in, ffn_conv_w, ffn_conv_b, ffn_w_out):
    B, S, D = x.shape
    tm = min(TILE_ROWS, S)
    depth = mix_pre_g.shape[0]
    n_a = pool_w.shape[0]
    F = ffn_w_out.shape[1]
    hkvd = N_KV_HEADS * HEAD_DIM
    assert S % tm == 0 and tm % (2 * BLOCK) == 0 and F % FFN_CHUNK == 0

    assert n_a >= 1 and depth % n_a == 0
    per = depth // n_a
    win2d, wout2d = ffn_w_in.reshape(depth * D, 2 * F), ffn_w_out.reshape(depth * F, D)
    win_bf, wout_bf = [], []
    wq_all, wo_all = _pair_layout_q(w_q).astype(BF16), w_o.astype(BF16)
    cos = sin = k = v = None
    for layer in range(depth):
        gpre, gpost = mix_pre_g[layer][None, :], mix_post_g[layer][None, :]
        if layer < n_a:
            x, wi, wo_ = _pool_layer(x, gpre, gpost, pool_w[layer].astype(BF16), pool_scale[layer][None, :],
                                     win2d, wout2d, layer, n_a, tm=tm)
            win_bf.append(wi.reshape(per, D, 2 * F))
            wout_bf.append(wo_.reshape(per, F, D))
        else:
            if layer == n_a:
                k, v, cos, sin = _kv_proj(x, kv_norm_g[None, :], positions,
                                          _dup_layout_k(w_kv[:, :hkvd]).astype(BF16),
                                          _dup_layout_v(w_kv[:, hkvd:]).astype(BF16), tm=tm)
            j = layer - n_a
            x = _attn_layer(x, gpre, gpost, cos, sin, k, v, wq_all, wo_all, sinks[j], layer=j, tm=tm)
        x = _ffn_layer(x, ffn_pre_g[layer][None, :], ffn_post_g[layer][None, :],
                       win_bf[layer // per], ffn_conv_w[layer], ffn_conv_b[layer][None, :], wout_bf[layer // per],
                       layer=layer % per, tm=tm)
    return x
```

```python
import functools
import math

import jax
import jax.numpy as jnp
from jax import lax
from jax.experimental import pallas as pl
from jax.experimental.pallas import tpu as pltpu

F32 = jnp.float32
BF16 = jnp.bfloat16

RMS_EPS = 1e-6
POOL_WINDOWS = (2, 4, 8, 16)
HEAD_DIM = 64
HALF = HEAD_DIM // 2
N_Q_HEADS = 16
N_KV_HEADS = 4
Q_PER_KV = N_Q_HEADS // N_KV_HEADS
WINDOW = 128
BLOCK = 128
ROPE_THETA = 10000.0
ATTN_SCALE = 1.0 / math.sqrt(HEAD_DIM)
NEG_INF = -1e30
CONV_WIDTH = 3
LANES = 128
SUBLANES = 8
STAGE_PAD = 8
FFN_CHUNK = 256
VMEM_LIMIT = 56 * 1024 * 1024

TILE_ROWS = 1024

GELU_C = math.sqrt(2.0 / math.pi)
LOG2E = math.log2(math.e)


def _rms(x, g):
    ms = jnp.mean(x * x, axis=-1, keepdims=True)
    return x * lax.rsqrt(ms + RMS_EPS) * g


def _const_spec(shape):
    nd = len(shape)
    return pl.BlockSpec(shape, lambda b, s: (0,) * nd, pipeline_mode=pl.Buffered(1))


def _layer_spec(shape, layer):
    nd = len(shape) - 1
    return pl.BlockSpec((None,) + tuple(shape[1:]), lambda b, s: (layer,) + (0,) * nd, pipeline_mode=pl.Buffered(1))


def _params():
    return pltpu.CompilerParams(dimension_semantics=("arbitrary", "arbitrary"),
                                vmem_limit_bytes=VMEM_LIMIT)


def _pool_kernel(x_ref, gpre_ref, gpost_ref, wp_ref, scale_ref, win_ref, wout_ref, o_ref, win_bf_ref, wout_bf_ref,
                 stage, hp, *hprev, tm, gc):
    win_bf_ref[...] = win_ref[...].astype(BF16)
    wout_bf_ref[...] = wout_ref[...].astype(BF16)
    s_idx = pl.program_id(1)
    D = hp.shape[1]
    Q = tm // SUBLANES
    P = Q + STAGE_PAD
    nslab = D // LANES

    @pl.when(s_idx == 0)
    def _():
        for prev in hprev:
            prev[...] = jnp.zeros(prev.shape, F32)

    x = x_ref[0]
    h = _rms(x, gpre_ref[...])
    for j in range(nslab):
        for i in range(SUBLANES):
            stage[j, i * P:i * P + Q, :] = h[i * Q:(i + 1) * Q, j * LANES:(j + 1) * LANES]
    for k in range(Q):
        hp[k * SUBLANES:(k + 1) * SUBLANES, :] = jnp.concatenate(
            [stage[j, pl.ds(k, SUBLANES, stride=P), :] for j in range(nslab)], axis=1)

    def shifted(t, sft, prev):
        rows, cols = sft * SUBLANES, t.shape[1]
        last = t[tm - rows:, :]
        sub = lax.broadcasted_iota(jnp.int32, (rows, cols), 0) % SUBLANES
        wrap = jnp.where(sub == SUBLANES - 1, prev[...], last)
        wrap = jnp.concatenate([pltpu.roll(wrap[m * SUBLANES:(m + 1) * SUBLANES, :], 1, 0) for m in range(sft)], axis=0)
        prev[...] = last
        return jnp.concatenate([wrap, t[:tm - rows, :]], axis=0)

    level, sums = hp[...], []
    for gi, (w, prev) in enumerate(zip(POOL_WINDOWS, hprev)):
        assert w == 2 ** (gi + 1)
        level = level + shifted(level, w // 2, prev)
        sums.append(level[:, :gc])
        level = level[:, gc:]

    first = (s_idx == 0).astype(F32)
    sub8 = lax.broadcasted_iota(jnp.int32, (SUBLANES, gc), 0)
    ys = []
    for gi, w in enumerate(POOL_WINDOWS):
        own = hp[:, gi * gc:(gi + 1) * gc]
        mean = sums[gi] * (1.0 / w)
        fix = [jnp.where(sub8 == 0, 1.0 + first * (w / (k + 1.0) - 1.0), 1.0) for k in range(w - 1)]
        head = jnp.concatenate(fix, axis=0) * mean[0:(w - 1) * SUBLANES, :]
        mean = jnp.concatenate([head, mean[(w - 1) * SUBLANES:, :]], axis=0)
        d = mean - own
        ys.append(jnp.dot(d.astype(BF16), wp_ref[gi], preferred_element_type=F32))
    m = jnp.concatenate(ys, axis=-1) * scale_ref[...]
    r = _rms(m, gpost_ref[...])
    for k in range(Q):
        for j in range(nslab):
            stage[j, pl.ds(k, SUBLANES, stride=P), :] = r[k * SUBLANES:(k + 1) * SUBLANES, j * LANES:(j + 1) * LANES]
    rn = jnp.concatenate([jnp.concatenate([stage[j, i * P:i * P + Q, :] for i in range(SUBLANES)], axis=0)
                          for j in range(nslab)], axis=1)
    o_ref[0] = x + rn


def _pool_layer(x, gpre, gpost, wp, scale, win, wout, part, nparts, *, tm):
    B, S, D = x.shape
    gc = D // len(POOL_WINDOWS)
    wmax = max(POOL_WINDOWS)
    assert tm // SUBLANES >= wmax
    ns = S // tm
    nsteps = B * ns
    xspec = pl.BlockSpec((1, tm, D), lambda b, s: (b, s, 0))

    def cast_specs(w):
        rows = w.shape[0] // nparts
        blk = rows // nsteps
        assert w.shape[0] % nparts == 0 and rows % nsteps == 0 and blk % (2 * SUBLANES) == 0
        return (pl.BlockSpec((blk, w.shape[1]), lambda b, s: (part * nsteps + b * ns + s, 0)),
                pl.BlockSpec((blk, w.shape[1]), lambda b, s: (b * ns + s, 0)),
                jax.ShapeDtypeStruct((rows, w.shape[1]), BF16))

    win_in, win_out, win_shape = cast_specs(win)
    wout_in, wout_out, wout_shape = cast_specs(wout)
    return pl.pallas_call(
        functools.partial(_pool_kernel, tm=tm, gc=gc),
        grid=(B, ns),
        in_specs=[xspec, _const_spec((1, D)), _const_spec((1, D)),
                  _const_spec(wp.shape), _const_spec((1, D)), win_in, wout_in],
        out_specs=[xspec, win_out, wout_out],
        out_shape=[jax.ShapeDtypeStruct(x.shape, x.dtype), win_shape, wout_shape],
        scratch_shapes=[pltpu.VMEM((D // LANES, tm + SUBLANES * STAGE_PAD, LANES), F32),
                        pltpu.VMEM((tm, D), F32)]
                       + [pltpu.VMEM((w // 2 * SUBLANES, D - gi * gc), F32) for gi, w in enumerate(POOL_WINDOWS)],
        compiler_params=_params(),
        name="pool_layer",
    )(x, gpre, gpost, wp, scale, win, wout)


def _gelu_gate(gate, val):
    inner = gate * (GELU_C + (GELU_C * 0.044715) * (gate * gate))
    return (gate * val) * (0.5 * jnp.tanh(inner) + 0.5)


def _ffn_kernel(x_ref, gpre_ref, gpost_ref, win_ref, cw_ref, cb_ref, wout_ref, o_ref,
                stage_in, stage_out, h_scr, act_scr, halo, *, tm, fc, nc):
    F = nc * fc
    D = wout_ref.shape[1]
    Q = tm // SUBLANES
    P = Q + STAGE_PAD
    nslab = D // LANES

    @pl.when(pl.program_id(1) == 0)
    def _():
        halo[...] = jnp.zeros(halo.shape, F32)

    x = x_ref[0]
    h = _rms(x, gpre_ref[...])
    for j in range(nslab):
        for i in range(SUBLANES):
            stage_in[j, i * P:i * P + Q, :] = h[i * Q:(i + 1) * Q, j * LANES:(j + 1) * LANES]
    for k in range(0, Q, 2):
        blk = [jnp.concatenate([stage_in[j, pl.ds(k + d, SUBLANES, stride=P), :] for j in range(nslab)], axis=1)
               for d in range(2)]
        h_scr[k * SUBLANES:(k + 2) * SUBLANES, :] = jnp.concatenate(blk, axis=0).astype(BF16)

    sub = lax.broadcasted_iota(jnp.int32, (SUBLANES, fc), 0)

    def conv(col):
        u = jnp.dot(h_scr[...], win_ref[:, col:col + fc], preferred_element_type=F32)
        last1, last2 = u[tm - SUBLANES:, :], u[tm - 2 * SUBLANES:tm - SUBLANES, :]
        b1 = pltpu.roll(jnp.where(sub == SUBLANES - 1, halo[0:SUBLANES, col:col + fc], last1), 1, 0)
        b2 = pltpu.roll(jnp.where(sub == SUBLANES - 1, halo[SUBLANES:2 * SUBLANES, col:col + fc], last2), 1, 0)
        halo[0:SUBLANES, col:col + fc] = last1
        halo[SUBLANES:2 * SUBLANES, col:col + fc] = last2
        um1 = jnp.concatenate([b1, u[:tm - SUBLANES, :]], axis=0)
        um2 = jnp.concatenate([b2, b1, u[:tm - 2 * SUBLANES, :]], axis=0)
        return (cw_ref[0:1, col:col + fc] * um2 + cw_ref[1:2, col:col + fc] * um1
                + cw_ref[2:3, col:col + fc] * u + cb_ref[:, col:col + fc])

    for c in range(nc):
        gate = conv(c * fc)
        val = conv(F + c * fc)
        act_scr[:, c * fc:(c + 1) * fc] = _gelu_gate(gate, val).astype(BF16)

    f = jnp.dot(act_scr[...], wout_ref[...], preferred_element_type=F32)
    for k in range(Q):
        for j in range(nslab):
            stage_out[j, pl.ds(k, SUBLANES, stride=P), :] = f[k * SUBLANES:(k + 1) * SUBLANES, j * LANES:(j + 1) * LANES]
    fn = jnp.concatenate([jnp.concatenate([stage_out[j, i * P:i * P + Q, :] for i in range(SUBLANES)], axis=0)
                          for j in range(nslab)], axis=1)
    o_ref[0] = x + _rms(fn, gpost_ref[...])


def _ffn_layer(x, gpre, gpost, win, cw, cb, wout, *, layer, tm):
    B, S, D = x.shape
    F = wout.shape[1]
    fc = FFN_CHUNK
    nc = F // fc
    assert cw.shape == (CONV_WIDTH, 2 * F) and tm % (2 * SUBLANES) == 0
    xspec = pl.BlockSpec((1, tm, D), lambda b, s: (b, s, 0))
    stage = pltpu.VMEM((D // LANES, tm + SUBLANES * STAGE_PAD, LANES), F32)
    return pl.pallas_call(
        functools.partial(_ffn_kernel, tm=tm, fc=fc, nc=nc),
        grid=(B, S // tm),
        in_specs=[xspec, _const_spec((1, D)), _const_spec((1, D)), _layer_spec(win.shape, layer),
                  _const_spec(cw.shape), _const_spec(cb.shape), _layer_spec(wout.shape, layer)],
        out_specs=xspec,
        out_shape=jax.ShapeDtypeStruct(x.shape, x.dtype),
        scratch_shapes=[stage, stage,
                        pltpu.VMEM((tm, D), BF16), pltpu.VMEM((tm, F), BF16),
                        pltpu.VMEM(((CONV_WIDTH - 1) * SUBLANES, 2 * F), F32)],
        compiler_params=_params(),
        name="ffn_layer",
    )(x, gpre, gpost, win, cw, cb, wout)


def _rope_tables(pos, invf, sign, tm):
    ngrp = LANES // HALF
    q4 = tm // ngrp
    grp = lax.broadcasted_iota(jnp.int32, (q4, LANES), 1) // HALF
    packed = jnp.zeros((q4, LANES), F32)
    for g in range(ngrp):
        packed = jnp.where(grp == g, pos[g * q4:(g + 1) * q4, :], packed)
    ang = packed * invf
    tables = []
    for table in (jnp.cos(ang), jnp.sin(ang)):
        rolled = [table] + [pltpu.roll(table, HALF * d, 1) for d in range(1, ngrp)]
        quarters = []
        for g in range(ngrp):
            out = rolled[(0 - g) % ngrp]
            for h in range(1, ngrp):
                out = jnp.where(grp == h, rolled[(h - g) % ngrp], out)
            quarters.append(out)
        tables.append(jnp.concatenate(quarters, axis=0))
    return tables[0], tables[1] * sign


def _rope(t, cos, sin):
    return t * cos + pltpu.roll(t, LANES // 2, 1) * sin


def _kv_kernel(x_ref, g_ref, pos_ref, invf_ref, sign_ref, wk_ref, wv_ref, k_ref, v_ref, cos_ref, sin_ref, *, tm):
    cos, sin = _rope_tables(pos_ref[0].astype(F32), invf_ref[...], sign_ref[...], tm)
    cos_ref[0] = cos
    sin_ref[0] = sin
    h = _rms(x_ref[0], g_ref[...]).astype(BF16)
    kd = jnp.dot(h, wk_ref[...], preferred_element_type=F32)
    for j in range(N_KV_HEADS):
        k_ref[0, :, j * LANES:(j + 1) * LANES] = _rope(kd[:, j * LANES:(j + 1) * LANES], cos, sin).astype(BF16)
    v_ref[0] = jnp.dot(h, wv_ref[...], preferred_element_type=F32).astype(BF16)


def _kv_proj(x, g, positions, wk, wv, *, tm):
    B, S, D = x.shape
    W = N_KV_HEADS * LANES
    inv_freq = 1.0 / (ROPE_THETA ** (jnp.arange(0, HEAD_DIM, 2, dtype=F32) / HEAD_DIM))
    invf = jnp.tile(inv_freq, LANES // HALF)[None, :]
    sign = jnp.where(jnp.arange(LANES) < LANES // 2, -1.0, 1.0).astype(F32)[None, :]
    tspec = pl.BlockSpec((1, tm, LANES), lambda b, s: (b, s, 0))
    ospec = pl.BlockSpec((1, tm, W), lambda b, s: (b, s, 0))
    return pl.pallas_call(
        functools.partial(_kv_kernel, tm=tm),
        grid=(B, S // tm),
        in_specs=[pl.BlockSpec((1, tm, D), lambda b, s: (b, s, 0)), _const_spec((1, D)),
                  pl.BlockSpec((1, tm, 1), lambda b, s: (b, s, 0)), _const_spec((1, LANES)), _const_spec((1, LANES)),
                  _const_spec(wk.shape), _const_spec(wv.shape)],
        out_specs=[ospec, ospec, tspec, tspec],
        out_shape=[jax.ShapeDtypeStruct((B, S, W), BF16)] * 2 + [jax.ShapeDtypeStruct((B, S, LANES), F32)] * 2,
        compiler_params=_params(),
        name="kv_proj",
    )(x, g, positions[:, :, None], invf, sign, wk, wv)


def _attn_kernel(sinks_ref, x_ref, gpre_ref, gpost_ref, cos_ref, sin_ref, kc_ref, kp_ref, vc_ref, vp_ref,
                 wq_ref, wo_ref, o_ref, qa, qb, o_scr, *, tm):
    s_idx = pl.program_id(1)
    nblk = tm // BLOCK
    D = wq_ref.shape[0]
    cw = D // N_KV_HEADS
    lane = lax.broadcasted_iota(jnp.int32, (1, LANES), 1)
    first_of_pair = (lane % HEAD_DIM) < HALF

    def pre_norm(r0, rows=BLOCK):
        return _rms(x_ref[0, pl.ds(r0, rows), :], gpre_ref[...]).astype(BF16)

    def q_piece(h, j, r0, qdsts):
        q = jnp.dot(h, wq_ref[:, j * cw:(j + 1) * cw], preferred_element_type=F32)
        for d, qdst in enumerate(qdsts):
            rd = r0 + d * BLOCK
            cos, sin = cos_ref[0, pl.ds(rd, BLOCK), :], sin_ref[0, pl.ds(rd, BLOCK), :]
            for p in range(cw // LANES):
                qr = _rope(q[d * BLOCK:(d + 1) * BLOCK, p * LANES:(p + 1) * LANES], cos, sin) * (ATTN_SCALE * LOG2E)
                qdst[Q_PER_KV * j + 2 * p] = jnp.where(first_of_pair, qr, 0.0).astype(BF16)
                qdst[Q_PER_KV * j + 2 * p + 1] = jnp.where(first_of_pair, 0.0, qr).astype(BF16)

    def post_norm(mo_parts, r0, rows):
        mo = jnp.concatenate(mo_parts, axis=-1)
        o_ref[0, pl.ds(r0, rows), :] = x_ref[0, pl.ds(r0, rows), :] + _rms(mo, gpost_ref[...])

    def key_rows(cur_ref, prev_ref, r0, has_prev, cols):
        if has_prev:
            return cur_ref[0, pl.ds(pl.multiple_of(r0 - BLOCK, BLOCK), 2 * BLOCK), cols]
        return jnp.concatenate([prev_ref[0, :, cols], cur_ref[0, 0:BLOCK, cols]], axis=0)

    qi = lax.broadcasted_iota(jnp.int32, (BLOCK, 2 * BLOCK), 0)
    kj = lax.broadcasted_iota(jnp.int32, (BLOCK, 2 * BLOCK), 1)
    rel = BLOCK + qi - kj
    band = (rel >= 0) & (rel < WINDOW)
    out_lo = lax.broadcasted_iota(jnp.int32, (BLOCK, LANES), 1) < HEAD_DIM

    def block_step(jb, qcur, qnexts=(), n_out=0, has_prev=True):
        r0 = pl.multiple_of(jb * BLOCK, BLOCK)
        rp = pl.multiple_of((jb - n_out) * BLOCK, BLOCK) if n_out else None
        rn = pl.multiple_of((jb + 1) * BLOCK, BLOCK) if qnexts else None
        o_prev = o_scr[pl.ds(rp, n_out * BLOCK), :] if n_out else None
        h_next = pre_norm(rn, len(qnexts) * BLOCK) if qnexts else None
        seq_start = (s_idx * tm + r0 == 0).astype(jnp.int32)
        valid = band & (kj >= BLOCK * seq_start)
        mo_parts = []

        def scores(j):
            qblk = jnp.concatenate([qcur[Q_PER_KV * j + i] for i in range(Q_PER_KV)], axis=0)
            kblk = key_rows(kc_ref, kp_ref, r0, has_prev, slice(j * LANES, (j + 1) * LANES))
            return lax.dot_general(qblk, kblk, (((1,), (1,)), ((), ())), preferred_element_type=F32)

        def projections(j):
            if n_out:
                mo_parts.append(jnp.dot(o_prev, wo_ref[:, j * cw:(j + 1) * cw], preferred_element_type=F32))
            if qnexts:
                q_piece(h_next, j, rn, qnexts)

        def values(j, sc):
            ps, sink_p = [], []
            for i in range(Q_PER_KV):
                sink = sinks_ref[Q_PER_KV * j + i] * LOG2E
                sh = jnp.where(valid, sc[i * BLOCK:(i + 1) * BLOCK], NEG_INF)
                m = jnp.maximum(jnp.max(sh, axis=-1, keepdims=True), sink)
                ps.append(jnp.exp2(sh - m).astype(BF16))
                sink_p.append(jnp.exp2(sink - m))
            vblk = jnp.concatenate([key_rows(vc_ref, vp_ref, r0, has_prev, slice(j * LANES, (j + 1) * LANES)),
                                    jnp.ones((2 * BLOCK, LANES), BF16)], axis=1)
            ov = jnp.dot(jnp.concatenate(ps, axis=0), vblk, preferred_element_type=F32)
            heads = []
            for i in range(Q_PER_KV):
                num = ov[i * BLOCK:(i + 1) * BLOCK, :LANES]
                den = ov[i * BLOCK:(i + 1) * BLOCK, LANES:] + sink_p[i]
                heads.append(num / den)
            for pp in range(Q_PER_KV // 2):
                col = (Q_PER_KV // 2 * j + pp) * LANES
                o_scr[pl.ds(r0, BLOCK), col:col + LANES] = jnp.where(out_lo, heads[2 * pp], heads[2 * pp + 1]).astype(BF16)

        sc = scores(0)
        for j in range(N_KV_HEADS):
            sc_next = scores(j + 1) if j + 1 < N_KV_HEADS else None
            projections(j)
            values(j, sc)
            sc = sc_next
        if n_out:
            post_norm(mo_parts, rp, n_out * BLOCK)

    h0 = pre_norm(0)
    for j in range(N_KV_HEADS):
        q_piece(h0, j, 0, (qa,))

    block_step(0, qa, qnexts=(qb,), has_prev=False)

    def pair_body(i, carry):
        block_step(2 * i + 1, qb, qnexts=(qa, qb))
        block_step(2 * i + 2, qa, n_out=2)
        return carry

    lax.fori_loop(0, nblk // 2 - 1, pair_body, 0)
    block_step(nblk - 1, qb, n_out=1)
    r_last = (nblk - 1) * BLOCK
    mo_last = jnp.dot(o_scr[r_last:r_last + BLOCK, :], wo_ref[...], preferred_element_type=F32)
    o_ref[0, r_last:r_last + BLOCK, :] = x_ref[0, r_last:r_last + BLOCK, :] + _rms(mo_last, gpost_ref[...])


def _attn_layer(x, gpre, gpost, cos, sin, k, v, wq, wo, sinks, *, layer, tm):
    B, S, D = x.shape
    W = N_KV_HEADS * LANES
    nbt = tm // BLOCK
    xspec = pl.BlockSpec((1, tm, D), lambda b, s: (b, s, 0))
    tspec = pl.BlockSpec((1, tm, LANES), lambda b, s: (b, s, 0))

    def cur(width):
        return pl.BlockSpec((1, tm, width), lambda b, s: (b, s, 0))

    def prev(width):
        return pl.BlockSpec((1, BLOCK, width), lambda b, s: (b, jnp.maximum(s * nbt - 1, 0), 0))

    return pl.pallas_call(
        functools.partial(_attn_kernel, tm=tm),
        grid=(B, S // tm),
        in_specs=[pl.BlockSpec(memory_space=pltpu.SMEM), xspec, _const_spec((1, D)), _const_spec((1, D)),
                  tspec, tspec, cur(W), prev(W), cur(W), prev(W),
                  _layer_spec(wq.shape, layer), _layer_spec(wo.shape, layer)],
        out_specs=xspec,
        out_shape=jax.ShapeDtypeStruct(x.shape, x.dtype),
        scratch_shapes=[pltpu.VMEM((N_Q_HEADS, BLOCK, LANES), BF16), pltpu.VMEM((N_Q_HEADS, BLOCK, LANES), BF16),
                        pltpu.VMEM((tm, D), BF16)],
        compiler_params=_params(),
        name="attn_layer",
    )(sinks, x, gpre, gpost, cos, sin, k, k, v, v, wq, wo)


def _pair_layout_q(wq):
    lead = wq.shape[:-1]
    w = wq.reshape(*lead, N_Q_HEADS // 2, 2, 2, HALF)
    w = jnp.swapaxes(w, -3, -2)
    return w.reshape(*lead, N_Q_HEADS * HEAD_DIM)


def _dup_layout_k(wk):
    D = wk.shape[0]
    w = wk.reshape(D, N_KV_HEADS, 2, 1, HALF)
    w = jnp.broadcast_to(w, (D, N_KV_HEADS, 2, 2, HALF))
    return w.reshape(D, N_KV_HEADS * LANES)


def _dup_layout_v(wv):
    D = wv.shape[0]
    w = wv.reshape(D, N_KV_HEADS, 1, HEAD_DIM)
    w = jnp.broadcast_to(w, (D, N_KV_HEADS, 2, HEAD_DIM))
    return w.reshape(D, N_KV_HEADS * LANES)


def kernel(x, positions, mix_pre_g, mix_post_g, pool_w, pool_scale, kv_norm_g, w_kv, w_q, w_o, sinks,
           ffn_pre_g, ffn_post_g, ffn_w_in, ffn_conv_w, ffn_conv_b, ffn_w_out):
    B, S, D = x.shape
    tm = min(TILE_ROWS, S)
    depth = mix_pre_g.shape[0]
    n_a = pool_w.shape[0]
    F = ffn_w_out.shape[1]
    hkvd = N_KV_HEADS * HEAD_DIM
    assert S % tm == 0 and tm % (2 * BLOCK) == 0 and F % FFN_CHUNK == 0

    assert n_a >= 1 and depth % n_a == 0
    per = depth // n_a
    win2d, wout2d = ffn_w_in.reshape(depth * D, 2 * F), ffn_w_out.reshape(depth * F, D)
    win_bf, wout_bf = [], []
    wq_all, wo_all = _pair_layout_q(w_q).astype(BF16), w_o.astype(BF16)
    cos = sin = k = v = None
    for layer in range(depth):
        gpre, gpost = mix_pre_g[layer][None, :], mix_post_g[layer][None, :]
        if layer < n_a:
            x, wi, wo_ = _pool_layer(x, gpre, gpost, pool_w[layer].astype(BF16), pool_scale[layer][None, :],
                                     win2d, wout2d, layer, n_a, tm=tm)
            win_bf.append(wi.reshape(per, D, 2 * F))
            wout_bf.append(wo_.reshape(per, F, D))
        else:
            if layer == n_a:
                k, v, cos, sin = _kv_proj(x, kv_norm_g[None, :], positions,
                                          _dup_layout_k(w_kv[:, :hkvd]).astype(BF16),
                                          _dup_layout_v(w_kv[:, hkvd:]).astype(BF16), tm=tm)
            j = layer - n_a
            x = _attn_layer(x, gpre, gpost, cos, sin, k, v, wq_all, wo_all, sinks[j], layer=j, tm=tm)
        x = _ffn_layer(x, ffn_pre_g[layer][None, :], ffn_post_g[layer][None, :],
                       win_bf[layer // per], ffn_conv_w[layer], ffn_conv_b[layer][None, :], wout_bf[layer // per],
                       layer=layer % per, tm=tm)
    return x
```
